```python
import jax, jax.numpy as jnp
from jax import lax
import numpy as np

D_MODEL = 1024
BATCH = 32
SEQ = 2048
DEPTH = 4

RNN_WIDTH = 1024
RNN_BLOCKS = 16
RNN_CONV = 4
LRU_C = 8.0
NSA_HEADS = 16
NSA_KV_HEADS = 4
HEAD_DIM = 64
GQA_REP = NSA_HEADS // NSA_KV_HEADS
CMP_BLOCK = 32
CMP_STRIDE = 16
CMP_HIDDEN = 128
SLC_BLOCK = 64
SLC_TOPK = 4
WINDOW = 256
Q_CHUNK = 64
FORCE_BONUS = 1.0e4
ROPE_THETA = 500000.0
ROPE_DIM = HEAD_DIM // 4
SC_WIDTH = D_MODEL
SC_CONV = 3
D_FF = 2816
FFN_CONV = 3
EPS = 1e-6
NEG = -1e30

AB_SPLITS = (RNN_WIDTH, RNN_WIDTH, NSA_HEADS * HEAD_DIM) + (NSA_KV_HEADS * HEAD_DIM,) * 6 + (3 * NSA_HEADS,)
AB_IN = sum(AB_SPLITS)
AB_MIX = RNN_WIDTH + NSA_HEADS * HEAD_DIM

kernel_name = 'hybrid_rglru_nsa_shortconv_trunk'


def rms_norm(x, g):
    xf = x.astype(jnp.float32)
    y = xf * lax.rsqrt(jnp.mean(xf * xf, axis=-1, keepdims=True) + EPS)
    return (y * g.astype(jnp.float32)).astype(x.dtype)


def causal_dwconv(x, w, b):
    K, C = w.shape
    y = lax.conv_general_dilated(x, w[:, None, :].astype(x.dtype), window_strides=(1,),
                                 padding=[(K - 1, 0)], dimension_numbers=('NWC', 'WIO', 'NWC'),
                                 feature_group_count=C)
    return y + b.astype(x.dtype)


def rope_partial(x, pos):
    half = ROPE_DIM // 2
    inv = jnp.power(jnp.float32(ROPE_THETA), -jnp.arange(half, dtype=jnp.float32) / half)
    ang = pos.astype(jnp.float32)[:, None] * inv[None, :]
    cos = jnp.cos(ang)[None, :, None, :]
    sin = jnp.sin(ang)[None, :, None, :]
    x1 = x[..., :half].astype(jnp.float32)
    x2 = x[..., half:ROPE_DIM].astype(jnp.float32)
    rot = jnp.concatenate([x1 * cos - x2 * sin, x2 * cos + x1 * sin], axis=-1).astype(x.dtype)
    return jnp.concatenate([rot, x[..., ROPE_DIM:]], axis=-1)


def rg_lru(x, w_r, b_r, w_i, b_i, lam):
    Bsz, T, W = x.shape
    xb = x.reshape(Bsz, T, RNN_BLOCKS, W // RNN_BLOCKS)
    r = jax.nn.sigmoid(jnp.einsum('btnc,ncd->btnd', xb, w_r).reshape(Bsz, T, W) + b_r)
    i = jax.nn.sigmoid(jnp.einsum('btnc,ncd->btnd', xb, w_i).reshape(Bsz, T, W) + b_i)
    log_a = -LRU_C * r.astype(jnp.float32) * jax.nn.softplus(-lam.astype(jnp.float32))
    a = jnp.exp(log_a)
    u = jnp.sqrt(-jnp.expm1(2.0 * log_a)) * (i * x).astype(jnp.float32)

    def combine(left, right):
        a1, b1 = left
        a2, b2 = right
        return a1 * a2, a2 * b1 + b2

    _, h = lax.associative_scan(combine, (a, u), axis=1)
    return h.astype(x.dtype)


def compress(t, pe, w1, w2):
    Bsz, T, G, hd = t.shape
    n_cmp = (T - CMP_BLOCK) // CMP_STRIDE + 1
    idx = np.arange(n_cmp)[:, None] * CMP_STRIDE + np.arange(CMP_BLOCK)[None, :]
    blk = t[:, idx] + pe[:, None, :].astype(t.dtype)
    blk = blk.transpose(0, 1, 3, 2, 4).reshape(Bsz, n_cmp, G, CMP_BLOCK * hd)
    return jax.nn.gelu(blk @ w1) @ w2, idx[:, -1]


def nsa(q, kc, vc, ks, vs, kw, vw, gate_logits, pe_k, pe_v, wk1, wk2, wv1, wv2):
    Bsz, T = q.shape[:2]
    G, R, hd = NSA_KV_HEADS, GQA_REP, HEAD_DIM
    scale = HEAD_DIM ** -0.5
    pos = jnp.arange(T)
    q_rot = rope_partial(q, pos)
    ks = rope_partial(ks, pos)
    kw = rope_partial(kw, pos)
    k_cmp, cmp_end = compress(kc, pe_k, wk1, wk2)
    v_cmp, _ = compress(vc, pe_v, wv1, wv2)
    n_cmp = k_cmp.shape[1]
    cmp_end = jnp.asarray(cmp_end)
    n_slc = T // SLC_BLOCK
    k_sel = min(SLC_TOPK, n_slc)
    cs = np.arange(n_cmp) * CMP_STRIDE
    js = np.arange(n_slc) * SLC_BLOCK
    overlap = jnp.asarray(((cs[:, None] < js[None, :] + SLC_BLOCK) &
                           (cs[:, None] + CMP_BLOCK > js[None, :])).astype(np.float32))
    ksb = ks.reshape(Bsz, n_slc, SLC_BLOCK, G, hd).transpose(0, 3, 1, 2, 4)
    vsb = vs.reshape(Bsz, n_slc, SLC_BLOCK, G, hd).transpose(0, 3, 1, 2, 4)
    kwp = jnp.pad(kw, ((0, 0), (WINDOW, 0), (0, 0), (0, 0)))
    vwp = jnp.pad(vw, ((0, 0), (WINDOW, 0), (0, 0), (0, 0)))
    bi = jnp.arange(Bsz)[:, None, None, None]
    gi = jnp.arange(G)[None, :, None, None]
    blk_ids = jnp.arange(n_slc)
    f32 = jnp.float32

    def chunk(args):
        qn, qr, gt, s = args
        t = s + jnp.arange(Q_CHUNK)
        qn5 = qn.reshape(Bsz, Q_CHUNK, G, R, hd)
        qr5 = qr.reshape(Bsz, Q_CHUNK, G, R, hd)
        sc = jnp.einsum('bqgrd,bngd->bgrqn', qn5, k_cmp, preferred_element_type=f32) * scale
        mask_c = cmp_end[None, :] <= t[:, None]
        p_c = jnp.where(mask_c, jax.nn.softmax(jnp.where(mask_c, sc, NEG), axis=-1), 0.0)
        o_c = jnp.einsum('bgrqn,bngd->bqgrd', p_c.astype(v_cmp.dtype), v_cmp, preferred_element_type=f32)
        imp = jnp.einsum('bgrqn,nj->bgqj', p_c, overlap)
        forced = (blk_ids[None, :] == 0) | (blk_ids[None, :] == (t // SLC_BLOCK)[:, None])
        imp = jnp.where(forced, imp + FORCE_BONUS, imp)
        imp = jnp.where(blk_ids[None, :] * SLC_BLOCK <= t[:, None], imp, NEG)
        _, sel = lax.top_k(imp, k_sel)
        kg = ksb[bi, gi, sel]
        vg = vsb[bi, gi, sel]
        ss = jnp.einsum('bqgrd,bgqkld->bgrqkl', qr5, kg, preferred_element_type=f32) * scale
        tok = sel[..., None] * SLC_BLOCK + jnp.arange(SLC_BLOCK)
        mask_s = (tok <= t[None, None, :, None, None])[:, :, None]
        ss = jnp.where(mask_s, ss, NEG)
        p_s = jax.nn.softmax(ss.reshape(ss.shape[:4] + (k_sel * SLC_BLOCK,)), axis=-1).reshape(ss.shape)
        o_s = jnp.einsum('bgrqkl,bgqkld->bqgrd', p_s.astype(vg.dtype), vg, preferred_element_type=f32)
        kwin = lax.dynamic_slice_in_dim(kwp, s, WINDOW + Q_CHUNK, axis=1)
        vwin = lax.dynamic_slice_in_dim(vwp, s, WINDOW + Q_CHUNK, axis=1)
        j = s - WINDOW + jnp.arange(WINDOW + Q_CHUNK)
        mask_w = (j[None, :] <= t[:, None]) & (j[None, :] > t[:, None] - WINDOW) & (j[None, :] >= 0)
        sw = jnp.einsum('bqgrd,bjgd->bgrqj', qr5, kwin, preferred_element_type=f32) * scale
        p_w = jax.nn.softmax(jnp.where(mask_w, sw, NEG), axis=-1)
        o_w = jnp.einsum('bgrqj,bjgd->bqgrd', p_w.astype(vwin.dtype), vwin, preferred_element_type=f32)
        g = jax.nn.sigmoid(gt.astype(f32)).reshape(Bsz, Q_CHUNK, G, R, 3)
        o = g[..., 0:1] * o_c + g[..., 1:2] * o_s + g[..., 2:3] * o_w
        return o.reshape(Bsz, Q_CHUNK, NSA_HEADS * hd).astype(q.dtype)

    n_chunk = T // Q_CHUNK
    to_chunks = lambda a: a.reshape((Bsz, n_chunk, Q_CHUNK) + a.shape[2:]).swapaxes(0, 1)
    starts = jnp.arange(n_chunk) * Q_CHUNK
    out = lax.map(chunk, (to_chunks(q), to_chunks(q_rot), to_chunks(gate_logits), starts))
    return out.swapaxes(0, 1).reshape(Bsz, T, NSA_HEADS * hd)


def mixer_ab(h, w_in, conv_w, conv_b, w_r, b_r, w_i, b_i, lam, pe_k, pe_v, wk1, wk2, wv1, wv2, w_out):
    Bsz, T, _ = h.shape
    splits = np.cumsum(AB_SPLITS)[:-1].tolist()
    xr, gr, q, kc, vc, ks, vs, kw, vw, gl = jnp.split(h @ w_in, splits, axis=-1)
    y_rnn = rg_lru(causal_dwconv(xr, conv_w, conv_b), w_r, b_r, w_i, b_i, lam) * jax.nn.gelu(gr)
    kv = lambda a: a.reshape(Bsz, T, NSA_KV_HEADS, HEAD_DIM)
    y_att = nsa(q.reshape(Bsz, T, NSA_HEADS, HEAD_DIM), kv(kc), kv(vc), kv(ks), kv(vs), kv(kw), kv(vw),
                gl.reshape(Bsz, T, NSA_HEADS, 3), pe_k, pe_v, wk1, wk2, wv1, wv2)
    return jnp.concatenate([y_rnn, y_att], axis=-1) @ w_out


def mixer_c(h, w_in, conv_w, conv_b, w_out):
    bg, cg, v = jnp.split(h @ w_in, 3, axis=-1)
    return (bg * causal_dwconv(cg * v, conv_w, conv_b)) @ w_out


def conv_ffn(h, w_gate, w_up, conv_w, conv_b, w_down):
    g = causal_dwconv(h @ w_gate, conv_w, conv_b)
    return (jax.nn.silu(g) * (h @ w_up)) @ w_down


def setup_inputs(seed: int = 0) -> dict:
    key = jax.random.key(seed)
    keys = iter(jax.random.split(key, 48))
    NE = (DEPTH + 1) // 2
    NO = DEPTH // 2
    f32 = jnp.float32

    def w(shape, fan_in, scale=1.0):
        return jax.random.normal(next(keys), shape, f32) * (scale * fan_in ** -0.5)

    def small(shape):
        return 0.02 * jax.random.normal(next(keys), shape, f32)

    def gain(shape):
        return 1.0 + small(shape)

    x = jax.random.normal(next(keys), (BATCH, SEQ, D_MODEL), f32)
    c = jax.random.normal(next(keys), (BATCH, D_MODEL), f32)
    a_target = jax.random.uniform(next(keys), (NE, RNN_WIDTH), f32, 0.9, 0.999)
    s = a_target ** (1.0 / LRU_C)
    lam = jnp.log(s) - jnp.log1p(-s)
    bw = RNN_WIDTH // RNN_BLOCKS
    return {
        'x': x,
        'c': c,
        'mod_w': w((DEPTH, D_MODEL, 6 * D_MODEL), D_MODEL, 0.5),
        'mod_b': small((DEPTH, 6 * D_MODEL)),
        'norm_mix_pre': gain((DEPTH, D_MODEL)),
        'norm_mix_post': gain((DEPTH, D_MODEL)),
        'norm_ffn_pre': gain((DEPTH, D_MODEL)),
        'norm_ffn_post': gain((DEPTH, D_MODEL)),
        'ab_w_in': w((NE, D_MODEL, AB_IN), D_MODEL),
        'ab_conv_w': w((NE, RNN_CONV, RNN_WIDTH), RNN_CONV),
        'ab_conv_b': small((NE, RNN_WIDTH)),
        'lru_w_r': w((NE, RNN_BLOCKS, bw, bw), bw),
        'lru_b_r': small((NE, RNN_WIDTH)),
        'lru_w_i': w((NE, RNN_BLOCKS, bw, bw), bw),
        'lru_b_i': small((NE, RNN_WIDTH)),
        'lru_lam': lam,
        'cmp_pe_k': small((NE, CMP_BLOCK, HEAD_DIM)),
        'cmp_pe_v': small((NE, CMP_BLOCK, HEAD_DIM)),
        'cmp_wk1': w((NE, CMP_BLOCK * HEAD_DIM, CMP_HIDDEN), CMP_BLOCK * HEAD_DIM),
        'cmp_wk2': w((NE, CMP_HIDDEN, HEAD_DIM), CMP_HIDDEN),
        'cmp_wv1': w((NE, CMP_BLOCK * HEAD_DIM, CMP_HIDDEN), CMP_BLOCK * HEAD_DIM),
        'cmp_wv2': w((NE, CMP_HIDDEN, HEAD_DIM), CMP_HIDDEN),
        'ab_w_out': w((NE, AB_MIX, D_MODEL), AB_MIX),
        'sc_w_in': w((NO, D_MODEL, 3 * SC_WIDTH), D_MODEL),
        'sc_conv_w': w((NO, SC_CONV, SC_WIDTH), SC_CONV),
        'sc_conv_b': small((NO, SC_WIDTH)),
        'sc_w_out': w((NO, SC_WIDTH, D_MODEL), SC_WIDTH),
        'ffn_w_gate': w((DEPTH, D_MODEL, D_FF), D_MODEL),
        'ffn_w_up': w((DEPTH, D_MODEL, D_FF), D_MODEL),
        'ffn_conv_w': w((DEPTH, FFN_CONV, D_FF), FFN_CONV),
        'ffn_conv_b': small((DEPTH, D_FF)),
        'ffn_w_down': w((DEPTH, D_FF, D_MODEL), D_FF),
    }


def reference(x, c, mod_w, mod_b, norm_mix_pre, norm_mix_post, norm_ffn_pre, norm_ffn_post,
              ab_w_in, ab_conv_w, ab_conv_b, lru_w_r, lru_b_r, lru_w_i, lru_b_i, lru_lam,
              cmp_pe_k, cmp_pe_v, cmp_wk1, cmp_wk2, cmp_wv1, cmp_wv2, ab_w_out,
              sc_w_in, sc_conv_w, sc_conv_b, sc_w_out,
              ffn_w_gate, ffn_w_up, ffn_conv_w, ffn_conv_b, ffn_w_down):
    c_act = jax.nn.silu(c)
    for i in range(DEPTH):
        mod = c_act @ mod_w[i] + mod_b[i]
        sh_m, sc_m, g_m, sh_f, sc_f, g_f = jnp.split(mod[:, None, :], 6, axis=-1)
        h = rms_norm(x, norm_mix_pre[i]) * (1.0 + sc_m) + sh_m
        if i % 2 == 0:
            e = i // 2
            y = mixer_ab(h, ab_w_in[e], ab_conv_w[e], ab_conv_b[e], lru_w_r[e], lru_b_r[e],
                         lru_w_i[e], lru_b_i[e], lru_lam[e], cmp_pe_k[e], cmp_pe_v[e],
                         cmp_wk1[e], cmp_wk2[e], cmp_wv1[e], cmp_wv2[e], ab_w_out[e])
        else:
            o = i // 2
            y = mixer_c(h, sc_w_in[o], sc_conv_w[o], sc_conv_b[o], sc_w_out[o])
        x = x + (1.0 + g_m) * rms_norm(y, norm_mix_post[i])
        h = rms_norm(x, norm_ffn_pre[i]) * (1.0 + sc_f) + sh_f
        y = conv_ffn(h, ffn_w_gate[i], ffn_w_up[i], ffn_conv_w[i], ffn_conv_b[i], ffn_w_down[i])
        x = x + (1.0 + g_f) * rms_norm(y, norm_ffn_post[i])
    return x
```

```python
import functools

import numpy as np
import jax
import jax.numpy as jnp
from jax import lax
from jax.experimental import pallas as pl
from jax.experimental.pallas import tpu as pltpu

F32 = jnp.float32
BF16 = jnp.bfloat16

RNN_BLOCKS = 16
LRU_C = 8.0
NSA_HEADS = 16
NSA_KV_HEADS = 4
HEAD_DIM = 64
GQA_REP = NSA_HEADS // NSA_KV_HEADS
CMP_BLOCK = 32
CMP_STRIDE = 16
SLC_BLOCK = 64
SLC_SHIFT = 6
SLC_TOPK = 4
WINDOW = 256
FORCE_BONUS = 1.0e4
ROPE_THETA = 500000.0
ROPE_DIM = HEAD_DIM // 4
EPS = 1e-6
NEG = -1e30

LANES = 128
SUBLANES = 8
VMEM_LIMIT_BYTES = 56 * 1024 * 1024

GROUP_COLS = 512
Q_COL0 = 2048
GROUP_COL0 = 3072
AB_COLS = GROUP_COL0 + NSA_KV_HEADS * GROUP_COLS

ROW_TILE = 512
Q_TILE = 256
FF_CHUNK = 256


def _sigmoid(x):
    return 1.0 / (1.0 + jnp.exp(-x))


def _rms(x, gain):
    return x * lax.rsqrt(jnp.mean(x * x, axis=-1, keepdims=True) + EPS) * gain


def _shift_rows(x, prev8, d):
    r = pltpu.roll(x, d, axis=0)
    p = pltpu.roll(prev8, d, axis=0)
    row = lax.broadcasted_iota(jnp.int32, prev8.shape, 0)
    head = jnp.where(row < d, p, r[0:SUBLANES])
    return jnp.concatenate([head, r[SUBLANES:]], axis=0)


def _params(*semantics):
    return pltpu.CompilerParams(dimension_semantics=semantics, vmem_limit_bytes=VMEM_LIMIT_BYTES)


def _resident(shape):
    nd = len(shape)
    return pl.BlockSpec(shape, lambda *_: (0,) * nd, pipeline_mode=pl.Buffered(1))


def _mod_kernel(c_ref, w_ref, b_ref, o_ref):
    c = c_ref[...]
    act = (c * _sigmoid(c)).astype(BF16)
    o_ref[0] = jnp.dot(act, w_ref[0].astype(BF16), preferred_element_type=F32) + b_ref[0]


def _modulation(c, mod_w, mod_b):
    depth, d, d6 = mod_w.shape
    bsz = c.shape[0]
    n = d6 // d
    return pl.pallas_call(
        _mod_kernel,
        grid=(depth, n),
        in_specs=[pl.BlockSpec((bsz, d), lambda i, j: (0, 0)),
                  pl.BlockSpec((1, d, d), lambda i, j: (i, 0, j)),
                  pl.BlockSpec((1, 1, d), lambda i, j: (i, 0, j))],
        out_specs=pl.BlockSpec((1, bsz, d), lambda i, j: (i, 0, j)),
        out_shape=jax.ShapeDtypeStruct((depth, bsz, d6), F32),
        compiler_params=_params("arbitrary", "arbitrary"),
        name="modulation",
    )(c, mod_w, mod_b.reshape(depth, 1, d6))


def _proj_kernel(x_ref, mod_ref, gain_ref, w_ref, o_ref, *, chunk):
    m = mod_ref[0]
    h = (_rms(x_ref[0], gain_ref[...]) * (1.0 + m[1:2]) + m[0:1]).astype(BF16)
    for j in range(o_ref.shape[-1] // chunk):
        sl = slice(j * chunk, (j + 1) * chunk)
        o_ref[0, :, sl] = jnp.dot(h, w_ref[:, sl], preferred_element_type=F32).astype(BF16)


def _proj(x, mod, gain, w):
    bsz, t, d = x.shape
    n = w.shape[1]
    tm = min(ROW_TILE, t)
    return pl.pallas_call(
        functools.partial(_proj_kernel, chunk=512),
        grid=(bsz, t // tm),
        in_specs=[pl.BlockSpec((1, tm, d), lambda b, i: (b, i, 0)),
                  pl.BlockSpec((1, 6, d), lambda b, i: (b, 0, 0)),
                  _resident((1, d)),
                  _resident((d, n))],
        out_specs=pl.BlockSpec((1, tm, n), lambda b, i: (b, i, 0)),
        out_shape=jax.ShapeDtypeStruct((bsz, t, n), BF16),
        compiler_params=_params("arbitrary", "arbitrary"),
        name="proj_in",
    )(x, mod, gain, w)


def _gcm_kernel(*refs, ffn, fc, shift_row):
    if ffn:
        x_ref, mod_ref, gpre_ref, gpost_ref, wa_ref, wb_ref, cw_ref, cb_ref, wd_ref, o_ref, carry_ref, acc_ref = refs
        wc_ref = None
    else:
        (x_ref, mod_ref, gpre_ref, gpost_ref, wa_ref, wb_ref, wc_ref, cw_ref, cb_ref, wd_ref, o_ref,
         carry_ref, acc_ref) = refs
    tm = x_ref.shape[1]

    @pl.when(pl.program_id(1) == 0)
    def _():
        carry_ref[...] = jnp.zeros_like(carry_ref)

    x = x_ref[0]
    m = mod_ref[0]
    h = (_rms(x, gpre_ref[...]) * (1.0 + m[shift_row + 1:shift_row + 2]) + m[shift_row:shift_row + 1]).astype(BF16)
    for c in range(wa_ref.shape[1] // fc):
        sl = slice(c * fc, (c + 1) * fc)
        pa = jnp.dot(h, wa_ref[:, sl], preferred_element_type=F32)
        pb = jnp.dot(h, wb_ref[:, sl], preferred_element_type=F32)
        cin = pa if ffn else pb * jnp.dot(h, wc_ref[:, sl], preferred_element_type=F32)
        prev = carry_ref[c]
        carry_ref[c] = cin[tm - SUBLANES:tm]
        w = cw_ref[:, sl]
        z = w[2:3] * cin + w[1:2] * _shift_rows(cin, prev, 1) + w[0:1] * _shift_rows(cin, prev, 2) + cb_ref[:, sl]
        act = (z * _sigmoid(z) * pb) if ffn else (pa * z)
        contrib = jnp.dot(act.astype(BF16), wd_ref[sl, :], preferred_element_type=F32)
        if c == 0:
            acc_ref[...] = contrib
        else:
            acc_ref[...] += contrib
    o_ref[0] = x + (1.0 + m[shift_row + 2:shift_row + 3]) * _rms(acc_ref[...], gpost_ref[...])


def _gated_conv_block(x, mod, gpre, gpost, w_ins, conv_w, conv_b, w_down, *, ffn):
    bsz, t, d = x.shape
    f = w_down.shape[0]
    tm = min(ROW_TILE, t)
    fc = FF_CHUNK
    in_specs = [pl.BlockSpec((1, tm, d), lambda b, i: (b, i, 0)),
                pl.BlockSpec((1, 6, d), lambda b, i: (b, 0, 0)),
                _resident((1, d)), _resident((1, d))]
    in_specs += [_resident((d, f)) for _ in w_ins]
    in_specs += [_resident(conv_w.shape), _resident((1, f)), _resident((f, d))]
    return pl.pallas_call(
        functools.partial(_gcm_kernel, ffn=ffn, fc=fc, shift_row=3 if ffn else 0),
        grid=(bsz, t // tm),
        in_specs=in_specs,
        out_specs=pl.BlockSpec((1, tm, d), lambda b, i: (b, i, 0)),
        out_shape=jax.ShapeDtypeStruct((bsz, t, d), F32),
        scratch_shapes=[pltpu.VMEM((f // fc, SUBLANES, fc), F32), pltpu.VMEM((tm, d), F32)],
        compiler_params=_params("arbitrary", "arbitrary"),
        name="ffn" if ffn else "short_conv_mixer",
    )(x, mod, gpre, gpost, *w_ins, conv_w, conv_b, w_down)


def _rglru_kernel(xr_ref, gr_ref, cw_ref, cb_ref, wri_ref, br_ref, bi_ref, lam_ref, o_ref,
                  xprev_ref, hprev_ref, a_ref, u_ref):
    tt, width = a_ref.shape

    @pl.when(pl.program_id(1) == 0)
    def _():
        xprev_ref[...] = jnp.zeros_like(xprev_ref)
        hprev_ref[...] = jnp.zeros_like(hprev_ref)

    xr = xr_ref[0].astype(F32)
    prev = xprev_ref[...]
    xprev_ref[...] = xr[tt - SUBLANES:tt]
    cw = cw_ref[...]
    xc = (cw[3:4] * xr + cw[2:3] * _shift_rows(xr, prev, 1) + cw[1:2] * _shift_rows(xr, prev, 2)
          + cw[0:1] * _shift_rows(xr, prev, 3) + cb_ref[...])
    neg_lam = -lam_ref[...]
    softplus = jnp.maximum(neg_lam, 0.0) + jnp.log1p(jnp.exp(-jnp.abs(neg_lam)))
    for j in range(width // LANES):
        sl = slice(j * LANES, (j + 1) * LANES)
        xs = xc[:, sl]
        ri = jnp.dot(xs.astype(BF16), wri_ref[j], preferred_element_type=F32)
        r = _sigmoid(ri[:, :LANES] + br_ref[:, sl])
        gate_i = _sigmoid(ri[:, LANES:] + bi_ref[:, sl])
        log_a = -LRU_C * r * softplus[:, sl]
        a_ref[:, sl] = jnp.exp(log_a)
        u_ref[:, sl] = jnp.sqrt(1.0 - jnp.exp(2.0 * log_a)) * (gate_i * xs)

    row = lax.broadcasted_iota(jnp.int32, (SUBLANES, width), 0)

    def body(k, h):
        r0 = pl.multiple_of(k * SUBLANES, SUBLANES)
        a = a_ref[pl.ds(r0, SUBLANES), :]
        b = u_ref[pl.ds(r0, SUBLANES), :]
        for d in (1, 2, 4):
            a_sh = jnp.where(row >= d, pltpu.roll(a, d, axis=0), 1.0)
            b_sh = jnp.where(row >= d, pltpu.roll(b, d, axis=0), 0.0)
            b = a * b_sh + b
            a = a * a_sh
        hh = b + a * h
        u_ref[pl.ds(r0, SUBLANES), :] = hh
        return hh[SUBLANES - 1:SUBLANES]

    hprev_ref[...] = lax.fori_loop(0, tt // SUBLANES, body, hprev_ref[...])
    o_ref[0] = (u_ref[...] * jax.nn.gelu(gr_ref[0].astype(F32))).astype(BF16)


def _rglru(p, conv_w, conv_b, w_ri, b_r, b_i, lam):
    bsz, t, _ = p.shape
    width = conv_w.shape[1]
    tt = min(ROW_TILE, t)
    return pl.pallas_call(
        _rglru_kernel,
        grid=(bsz, t // tt),
        in_specs=[pl.BlockSpec((1, tt, width), lambda b, i: (b, i, 0)),
                  pl.BlockSpec((1, tt, width), lambda b, i: (b, i, 1)),
                  _resident(conv_w.shape), _resident((1, width)), _resident(w_ri.shape),
                  _resident((1, width)), _resident((1, width)), _resident((1, width))],
        out_specs=pl.BlockSpec((1, tt, width), lambda b, i: (b, i, 0)),
        out_shape=jax.ShapeDtypeStruct((bsz, t, width), BF16),
        scratch_shapes=[pltpu.VMEM((SUBLANES, width), F32), pltpu.VMEM((1, width), F32),
                        pltpu.VMEM((tt, width), F32), pltpu.VMEM((tt, width), F32)],
        compiler_params=_params("arbitrary", "arbitrary"),
        name="rglru",
    )(p, p, conv_w, conv_b, w_ri, b_r, b_i, lam)


def _compress_kernel(rk_ref, rv_ref, pek_ref, pev_ref, wk1_ref, wk2_ref, wv1_ref, wv2_ref, ok_ref, ov_ref, *, n_blk):
    tm = rk_ref.shape[0]
    half = rk_ref.shape[1]
    n_in_group = lax.broadcasted_iota(jnp.int32, (tm, 1), 0) & (n_blk - 1)

    def one(r_ref, pe_ref, w1_ref, w2_ref, o_ref):
        r = r_ref[...].astype(F32)
        top = jnp.dot((r + pe_ref[0:1]).astype(BF16), w1_ref[0:half], preferred_element_type=F32)
        bot = jnp.dot((r + pe_ref[1:2]).astype(BF16), w1_ref[half:], preferred_element_type=F32)
        nxt = pltpu.roll(bot, tm - 1, axis=0)
        hid = top + jnp.where(n_in_group == n_blk - 1, 0.0, nxt)
        o_ref[...] = jnp.dot(jax.nn.gelu(hid).astype(BF16), w2_ref[...], preferred_element_type=F32).astype(BF16)

    one(rk_ref, pek_ref, wk1_ref, wk2_ref, ok_ref)
    one(rv_ref, pev_ref, wv1_ref, wv2_ref, ov_ref)


def _compress(rk, rv, pe_k, pe_v, wk1, wk2, wv1, wv2, n_blk):
    rows, half = rk.shape
    tm = min(ROW_TILE, rows)
    row_spec = pl.BlockSpec((tm, half), lambda i: (i, 0))
    out_spec = pl.BlockSpec((tm, HEAD_DIM), lambda i: (i, 0))
    out = jax.ShapeDtypeStruct((rows, HEAD_DIM), BF16)
    return pl.pallas_call(
        functools.partial(_compress_kernel, n_blk=n_blk),
        grid=(rows // tm,),
        in_specs=[row_spec, row_spec, _resident(pe_k.shape), _resident(pe_v.shape),
                  _resident(wk1.shape), _resident(wk2.shape), _resident(wv1.shape), _resident(wv2.shape)],
        out_specs=[out_spec, out_spec],
        out_shape=[out, out],
        compiler_params=_params("arbitrary"),
        name="nsa_compress",
    )(rk, rv, pe_k, pe_v, wk1, wk2, wv1, wv2)


def _rope(x, cos, sin_lo, sin_hi):
    w = x.shape[1]
    half = ROPE_DIM // 2
    return x * cos + pltpu.roll(x, half, axis=1) * sin_hi + pltpu.roll(x, w - half, axis=1) * sin_lo


def _nsa_kernel(q_ref, kv_ref, kc_ref, vc_ref, cos_ref, slo_ref, shi_ref, o_ref,
                ks_s, vs_s, kw_s, vw_s, m_s, l_s, acc_s):
    tq = q_ref.shape[1]
    hd = HEAD_DIM
    rep = GQA_REP
    n_cmp_pad = kc_ref.shape[2]
    n_slc = ks_s.shape[0] // SLC_BLOCK
    qi = pl.program_id(2)
    t0 = pl.multiple_of(qi * tq, tq)
    cos, slo, shi = cos_ref[...], slo_ref[...], shi_ref[...]

    kv = kv_ref[0]
    lane = lax.broadcasted_iota(jnp.int32, (1, 2 * hd), 1)
    cos_k = jnp.where(lane < hd, cos[:, :2 * hd], 1.0)
    slo_k = jnp.where(lane < hd, slo[:, :2 * hd], 0.0)
    shi_k = jnp.where(lane < hd, shi[:, :2 * hd], 0.0)
    sel_pair = _rope(kv[:, 0:2 * hd].astype(F32), cos_k, slo_k, shi_k).astype(BF16)
    win_pair = _rope(kv[:, 2 * hd:4 * hd].astype(F32), cos_k, slo_k, shi_k).astype(BF16)
    ks_s[pl.ds(t0, tq), :] = sel_pair[:, :hd]
    vs_s[pl.ds(t0, tq), :] = sel_pair[:, hd:]
    kw_s[pl.ds(t0, tq), :] = win_pair[:, :hd]
    vw_s[pl.ds(t0, tq), :] = win_pair[:, hd:]

    @pl.when(qi == 0)
    def _():
        kw_s[pl.ds(tq, tq), :] = jnp.zeros((tq, hd), BF16)
        vw_s[pl.ds(tq, tq), :] = jnp.zeros((tq, hd), BF16)

    scale = hd ** -0.5
    q = q_ref[0].astype(F32) * scale
    q_rot = _rope(q, cos, slo, shi).astype(BF16)
    qb = q.astype(BF16)
    t = t0 + lax.broadcasted_iota(jnp.int32, (tq, 1), 0)

    kc = kc_ref[0, 0]
    vc = vc_ref[0, 0]
    n_idx = lax.broadcasted_iota(jnp.int32, (1, n_cmp_pad), 1)
    mask_c = (n_idx * CMP_STRIDE + (CMP_BLOCK - 1)) <= t
    p_sum = jnp.zeros((tq, n_cmp_pad), F32)
    o_cmp = []
    for r in range(rep):
        s = lax.dot_general(qb[:, r * hd:(r + 1) * hd], kc, (((1,), (1,)), ((), ())), preferred_element_type=F32)
        s = jnp.where(mask_c, s, NEG)
        e = jnp.where(mask_c, jnp.exp(s - jnp.max(s, axis=-1, keepdims=True)), 0.0)
        den = jnp.sum(e, axis=-1, keepdims=True)
        p = e * jnp.where(den > 0.0, 1.0 / den, 0.0)
        p_sum = p_sum + p
        o_cmp.append(jnp.dot(p.astype(BF16), vc, preferred_element_type=F32))

    n_col = lax.broadcasted_iota(jnp.int32, (n_cmp_pad, n_slc), 0) * CMP_STRIDE
    j_col = lax.broadcasted_iota(jnp.int32, (n_cmp_pad, n_slc), 1) * SLC_BLOCK
    overlap = jnp.where((n_col < j_col + SLC_BLOCK) & (n_col + CMP_BLOCK > j_col), 1.0, 0.0)
    imp = jnp.dot(p_sum, overlap, preferred_element_type=F32, precision=lax.Precision.HIGHEST)
    blk = lax.broadcasted_iota(jnp.int32, (1, n_slc), 1)
    forced = (blk == 0) | (blk == (t >> SLC_SHIFT))
    imp = jnp.where(forced, imp + FORCE_BONUS, imp)
    imp = jnp.where(blk * SLC_BLOCK <= t, imp, NEG)
    blk_f = blk.astype(F32)
    sel = jnp.zeros((tq, n_slc), F32)
    for _ in range(SLC_TOPK):
        mx = jnp.max(imp, axis=-1, keepdims=True)
        first = jnp.min(jnp.where(imp == mx, blk_f, float(n_slc)), axis=-1, keepdims=True)
        pick = blk_f == first
        sel = jnp.where(pick, 1.0, sel)
        imp = jnp.where(pick, -jnp.inf, imp)
    sel_b = sel.astype(BF16)

    q4 = jnp.concatenate([q_rot[:, r * hd:(r + 1) * hd] for r in range(rep)], axis=0)
    m_s[...] = jnp.full(m_s.shape, NEG, F32)
    l_s[...] = jnp.zeros(l_s.shape, F32)
    acc_s[...] = jnp.zeros(acc_s.shape, F32)
    blk_row = lax.broadcasted_iota(jnp.int32, (n_slc, tq), 0)

    def sel_step(kt, carry):
        k0 = pl.multiple_of(kt * tq, tq)
        keys = k0 + lax.broadcasted_iota(jnp.int32, (1, tq), 1)
        expand = jnp.where((keys >> SLC_SHIFT) == blk_row, 1.0, 0.0).astype(BF16)
        chosen = jnp.dot(sel_b, expand, preferred_element_type=F32)
        bias = jnp.where((chosen > 0.5) & (keys <= t), 0.0, NEG)
        s = lax.dot_general(q4, ks_s[pl.ds(k0, tq), :], (((1,), (1,)), ((), ())), preferred_element_type=F32)
        s = s + jnp.concatenate([bias] * rep, axis=0)
        m_old = m_s[...]
        m_new = jnp.maximum(m_old, jnp.max(s, axis=-1, keepdims=True))
        alpha = jnp.exp(m_old - m_new)
        p = jnp.exp(s - m_new)
        l_s[...] = alpha * l_s[...] + jnp.sum(p, axis=-1, keepdims=True)
        acc_s[...] = alpha * acc_s[...] + jnp.dot(p.astype(BF16), vs_s[pl.ds(k0, tq), :], preferred_element_type=F32)
        m_s[...] = m_new
        return carry

    lax.fori_loop(0, qi + 1, sel_step, 0)
    o_sel = acc_s[...] / l_s[...]

    w0 = pl.multiple_of(jnp.maximum(qi - 1, 0) * tq, tq)
    keys_w = w0 + lax.broadcasted_iota(jnp.int32, (1, 2 * tq), 1)
    bias_w = jnp.where((keys_w <= t) & (keys_w > t - WINDOW), 0.0, NEG)
    sw = lax.dot_general(q4, kw_s[pl.ds(w0, 2 * tq), :], (((1,), (1,)), ((), ())), preferred_element_type=F32)
    sw = sw + jnp.concatenate([bias_w] * rep, axis=0)
    ew = jnp.exp(sw - jnp.max(sw, axis=-1, keepdims=True))
    pw = ew / jnp.sum(ew, axis=-1, keepdims=True)
    o_win = jnp.dot(pw.astype(BF16), vw_s[pl.ds(w0, 2 * tq), :], preferred_element_type=F32)

    gate = _sigmoid(kv[:, 4 * hd:4 * hd + LANES].astype(F32))
    outs = []
    for r in range(rep):
        rows = slice(r * tq, (r + 1) * tq)
        outs.append(gate[:, 3 * r:3 * r + 1] * o_cmp[r] + gate[:, 3 * r + 1:3 * r + 2] * o_sel[rows]
                    + gate[:, 3 * r + 2:3 * r + 3] * o_win[rows])
    o_ref[0] = jnp.concatenate(outs, axis=1).astype(BF16)


def _nsa(p, k_cmp, v_cmp, cos, sin_lo, sin_hi):
    bsz, t, _ = p.shape
    g = NSA_KV_HEADS
    tq = min(Q_TILE, t // 2)
    qw = GQA_REP * HEAD_DIM
    assert WINDOW <= tq and t % tq == 0 and t // SLC_BLOCK >= SLC_TOPK
    tab = pl.BlockSpec((tq, qw), lambda b, h, i: (i, 0))
    cmp_spec = pl.BlockSpec((1, 1) + k_cmp.shape[2:], lambda b, h, i: (b, h, 0, 0))
    return pl.pallas_call(
        _nsa_kernel,
        grid=(bsz, g, t // tq),
        in_specs=[pl.BlockSpec((1, tq, qw), lambda b, h, i: (b, i, Q_COL0 // qw + h)),
                  pl.BlockSpec((1, tq, GROUP_COLS), lambda b, h, i: (b, i, GROUP_COL0 // GROUP_COLS + h)),
                  cmp_spec, cmp_spec, tab, tab, tab],
        out_specs=pl.BlockSpec((1, tq, qw), lambda b, h, i: (b, i, h)),
        out_shape=jax.ShapeDtypeStruct((bsz, t, NSA_HEADS * HEAD_DIM), BF16),
        scratch_shapes=[pltpu.VMEM((t, HEAD_DIM), BF16) for _ in range(4)]
        + [pltpu.VMEM((GQA_REP * tq, 1), F32), pltpu.VMEM((GQA_REP * tq, 1), F32),
           pltpu.VMEM((GQA_REP * tq, HEAD_DIM), F32)],
        compiler_params=_params("arbitrary", "arbitrary", "arbitrary"),
        name="nsa",
    )(p, p, k_cmp, v_cmp, cos, sin_lo, sin_hi)


def _out_kernel(x_ref, ya_ref, yb_ref, mod_ref, gain_ref, wa_ref, wb_ref, o_ref):
    y = (jnp.dot(ya_ref[0], wa_ref[...], preferred_element_type=F32)
         + jnp.dot(yb_ref[0], wb_ref[...], preferred_element_type=F32))
    m = mod_ref[0]
    o_ref[0] = x_ref[0] + (1.0 + m[2:3]) * _rms(y, gain_ref[...])


def _out_proj(x, ya, yb, mod, gain, wa, wb):
    bsz, t, d = x.shape
    tm = min(ROW_TILE, t)
    row = lambda w: pl.BlockSpec((1, tm, w), lambda b, i: (b, i, 0))
    return pl.pallas_call(
        _out_kernel,
        grid=(bsz, t // tm),
        in_specs=[row(d), row(ya.shape[2]), row(yb.shape[2]), pl.BlockSpec((1, 6, d), lambda b, i: (b, 0, 0)),
                  _resident((1, d)), _resident(wa.shape), _resident(wb.shape)],
        out_specs=row(d),
        out_shape=jax.ShapeDtypeStruct((bsz, t, d), F32),
        compiler_params=_params("arbitrary", "arbitrary"),
        name="mixer_out",
    )(x, ya, yb, mod, gain, wa, wb)


def _ab_column_order():
    hd, g = HEAD_DIM, NSA_KV_HEADS
    kv0 = GROUP_COL0
    src = {name: kv0 + n * g * hd for n, name in enumerate(("kc", "vc", "ks", "vs", "kw", "vw"))}
    gates0 = kv0 + 6 * g * hd
    cols = list(range(GROUP_COL0))
    for grp in range(g):
        blockcols = []
        for name in ("ks", "vs", "kw", "vw"):
            blockcols += list(range(src[name] + grp * hd, src[name] + (grp + 1) * hd))
        n_gate = 3 * GQA_REP
        blockcols += list(range(gates0 + grp * n_gate, gates0 + (grp + 1) * n_gate)) + [-1] * (LANES - n_gate)
        for name in ("kc", "vc"):
            blockcols += list(range(src[name] + grp * hd, src[name] + (grp + 1) * hd))
        cols += blockcols
    return np.asarray(cols, np.int32)


def _rope_tables(t):
    half = ROPE_DIM // 2
    inv = jnp.power(jnp.float32(ROPE_THETA), -jnp.arange(half, dtype=F32) / half)
    ang = jnp.arange(t, dtype=F32)[:, None] * inv[None, :]
    cos, sin = jnp.cos(ang), jnp.sin(ang)
    rest = HEAD_DIM - ROPE_DIM
    one = jnp.ones((t, rest), F32)
    zero = jnp.zeros((t, rest), F32)
    zh = jnp.zeros((t, half), F32)
    cos_h = jnp.concatenate([cos, cos, one], axis=1)
    sin_lo = jnp.concatenate([-sin, zh, zero], axis=1)
    sin_hi = jnp.concatenate([zh, sin, zero], axis=1)
    tile = lambda a: jnp.tile(a, (1, GQA_REP))
    return tile(cos_h), tile(sin_lo), tile(sin_hi)


def _gate_weights(w_r, w_i):
    nb, bw, _ = w_r.shape
    per = LANES // bw
    def slabs(w):
        w = w.reshape(nb // per, per, bw, bw)
        eye = jnp.eye(per, dtype=w.dtype)
        return jnp.einsum("spcd,pq->spcqd", w, eye).reshape(nb // per, LANES, LANES)
    return jnp.concatenate([slabs(w_r), slabs(w_i)], axis=2).astype(BF16)


def kernel(x, c, mod_w, mod_b, norm_mix_pre, norm_mix_post, norm_ffn_pre, norm_ffn_post, ab_w_in, ab_conv_w,
           ab_conv_b, lru_w_r, lru_b_r, lru_w_i, lru_b_i, lru_lam, cmp_pe_k, cmp_pe_v, cmp_wk1, cmp_wk2, cmp_wv1,
           cmp_wv2, ab_w_out, sc_w_in, sc_conv_w, sc_conv_b, sc_w_out, ffn_w_gate, ffn_w_up, ffn_conv_w,
           ffn_conv_b, ffn_w_down):
    bsz, t, d = x.shape
    depth = mod_w.shape[0]
    g, hd = NSA_KV_HEADS, HEAD_DIM
    n_blk = t // CMP_STRIDE
    row2 = lambda a: a.reshape(1, -1)

    mod_all = _modulation(c, mod_w, mod_b).reshape(depth, bsz, 6, d)
    order = _ab_column_order()
    cos, sin_lo, sin_hi = _rope_tables(t)
    half_blk = CMP_STRIDE * hd

    for i in range(depth):
        mod = mod_all[i]
        if i % 2 == 0:
            e = i // 2
            w_in = jnp.concatenate([ab_w_in[e], jnp.zeros((d, 1), F32)], axis=1)[:, order].astype(BF16)
            p = _proj(x, mod, row2(norm_mix_pre[i]), w_in)
            y_rnn = _rglru(p, ab_conv_w[e], row2(ab_conv_b[e]), _gate_weights(lru_w_r[e], lru_w_i[e]),
                           row2(lru_b_r[e]), row2(lru_b_i[e]), row2(lru_lam[e]))
            grp = p[:, :, GROUP_COL0:].reshape(bsz, t, g, GROUP_COLS)
            to_rows = lambda a: a.transpose(0, 2, 1, 3).reshape(bsz * g * n_blk, half_blk)
            rk = to_rows(grp[..., GROUP_COLS - 2 * hd:GROUP_COLS - hd])
            rv = to_rows(grp[..., GROUP_COLS - hd:])
            k_cmp, v_cmp = _compress(rk, rv, cmp_pe_k[e].reshape(2, half_blk), cmp_pe_v[e].reshape(2, half_blk),
                                     cmp_wk1[e].astype(BF16), cmp_wk2[e].astype(BF16),
                                     cmp_wv1[e].astype(BF16), cmp_wv2[e].astype(BF16), n_blk)
            y_att = _nsa(p, k_cmp.reshape(bsz, g, n_blk, hd), v_cmp.reshape(bsz, g, n_blk, hd), cos, sin_lo, sin_hi)
            w_out = ab_w_out[e].astype(BF16)
            x = _out_proj(x, y_rnn, y_att, mod, row2(norm_mix_post[i]), w_out[:y_rnn.shape[2]], w_out[y_rnn.shape[2]:])
        else:
            o = i // 2
            w_in = sc_w_in[o].astype(BF16)
            f = sc_w_out.shape[1]
            x = _gated_conv_block(x, mod, row2(norm_mix_pre[i]), row2(norm_mix_post[i]),
                                  [w_in[:, :f], w_in[:, f:2 * f], w_in[:, 2 * f:]],
                                  sc_conv_w[o], row2(sc_conv_b[o]), sc_w_out[o].astype(BF16), ffn=False)
        x = _gated_conv_block(x, mod, row2(norm_ffn_pre[i]), row2(norm_ffn_post[i]),
                              [ffn_w_gate[i].astype(BF16), ffn_w_up[i].astype(BF16)],
                              ffn_conv_w[i], row2(ffn_conv_b[i]), ffn_w_down[i].astype(BF16), ffn=True)
    return x
```

```python
import functools

import numpy as np
import jax
import jax.numpy as jnp
from jax import lax
from jax.experimental import pallas as pl
from jax.experimental.pallas import tpu as pltpu

F32 = jnp.float32
BF16 = jnp.bfloat16

RNN_BLOCKS = 16
LRU_C = 8.0
NSA_HEADS = 16
NSA_KV_HEADS = 4
HEAD_DIM = 64
GQA_REP = NSA_HEADS // NSA_KV_HEADS
CMP_BLOCK = 32
CMP_STRIDE = 16
SLC_BLOCK = 64
SLC_SHIFT = 6
SLC_TOPK = 4
WINDOW = 256
FORCE_BONUS = 1.0e4
ROPE_THETA = 500000.0
ROPE_DIM = HEAD_DIM // 4
EPS = 1e-6
NEG = -1e30

LANES = 128
SUBLANES = 8
VMEM_LIMIT_BYTES = 56 * 1024 * 1024

GROUP_COLS = 384
Q_COL0 = 2048
GROUP_COL0 = 3072
CMP_COL0 = GROUP_COL0 + NSA_KV_HEADS * GROUP_COLS
CMP_COLS = 2 * NSA_KV_HEADS * HEAD_DIM

ROW_TILE = 512
Q_TILE = 256
FF_CHUNK = 256


def _sigmoid(x):
    return 1.0 / (1.0 + jnp.exp(-x))


def _rms(x, gain):
    return x * lax.rsqrt(jnp.mean(x * x, axis=-1, keepdims=True) + EPS) * gain


def _shift_rows(x, prev8, d):
    r = pltpu.roll(x, d, axis=0)
    p = pltpu.roll(prev8, d, axis=0)
    row = lax.broadcasted_iota(jnp.int32, prev8.shape, 0)
    head = jnp.where(row < d, p, r[0:SUBLANES])
    return jnp.concatenate([head, r[SUBLANES:]], axis=0)


def _params(*semantics):
    return pltpu.CompilerParams(dimension_semantics=semantics, vmem_limit_bytes=VMEM_LIMIT_BYTES)


def _resident(shape):
    nd = len(shape)
    return pl.BlockSpec(shape, lambda *_: (0,) * nd, pipeline_mode=pl.Buffered(1))


def _mod_kernel(c_ref, w_ref, b_ref, o_ref):
    c = c_ref[...]
    act = (c * _sigmoid(c)).astype(BF16)
    o_ref[0] = jnp.dot(act, w_ref[0].astype(BF16), preferred_element_type=F32) + b_ref[0]


def _modulation(c, mod_w, mod_b):
    depth, d, d6 = mod_w.shape
    bsz = c.shape[0]
    n = d6 // d
    return pl.pallas_call(
        _mod_kernel,
        grid=(depth, n),
        in_specs=[pl.BlockSpec((bsz, d), lambda i, j: (0, 0)),
                  pl.BlockSpec((1, d, d), lambda i, j: (i, 0, j)),
                  pl.BlockSpec((1, 1, d), lambda i, j: (i, 0, j))],
        out_specs=pl.BlockSpec((1, bsz, d), lambda i, j: (i, 0, j)),
        out_shape=jax.ShapeDtypeStruct((depth, bsz, d6), F32),
        compiler_params=_params("arbitrary", "arbitrary"),
        name="modulation",
    )(c, mod_w, mod_b.reshape(depth, 1, d6))


def _proj_kernel(x_ref, mod_ref, gain_ref, w_ref, o_ref, oc_ref, *, chunk):
    m = mod_ref[0]
    h = (_rms(x_ref[0], gain_ref[...]) * (1.0 + m[1:2]) + m[0:1]).astype(BF16)
    n_main = o_ref.shape[-1]
    for j in range(n_main // chunk):
        sl = slice(j * chunk, (j + 1) * chunk)
        o_ref[0, :, sl] = jnp.dot(h, w_ref[:, sl], preferred_element_type=F32).astype(BF16)
    cmp = jnp.dot(h, w_ref[:, n_main:], preferred_element_type=F32).astype(BF16)
    for j in range(oc_ref.shape[1]):
        oc_ref[0, j] = cmp[:, j * HEAD_DIM:(j + 1) * HEAD_DIM]


def _proj(x, mod, gain, w):
    bsz, t, d = x.shape
    n = w.shape[1] - CMP_COLS
    tm = min(ROW_TILE, t)
    return pl.pallas_call(
        functools.partial(_proj_kernel, chunk=512),
        grid=(bsz, t // tm),
        in_specs=[pl.BlockSpec((1, tm, d), lambda b, i: (b, i, 0)),
                  pl.BlockSpec((1, 6, d), lambda b, i: (b, 0, 0)),
                  _resident((1, d)),
                  _resident(w.shape)],
        out_specs=[pl.BlockSpec((1, tm, n), lambda b, i: (b, i, 0)),
                   pl.BlockSpec((1, CMP_COLS // HEAD_DIM, tm, HEAD_DIM), lambda b, i: (b, 0, i, 0))],
        out_shape=[jax.ShapeDtypeStruct((bsz, t, n), BF16),
                   jax.ShapeDtypeStruct((bsz, CMP_COLS // HEAD_DIM, t, HEAD_DIM), BF16)],
        compiler_params=_params("arbitrary", "arbitrary"),
        name="proj_in",
    )(x, mod, gain, w)


def _gcm_kernel(*refs, ffn, fc, shift_row):
    if ffn:
        x_ref, mod_ref, gpre_ref, gpost_ref, wa_ref, wb_ref, cw_ref, cb_ref, wd_ref, o_ref, carry_ref, acc_ref = refs
        wc_ref = None
    else:
        (x_ref, mod_ref, gpre_ref, gpost_ref, wa_ref, wb_ref, wc_ref, cw_ref, cb_ref, wd_ref, o_ref,
         carry_ref, acc_ref) = refs
    tm = x_ref.shape[1]

    @pl.when(pl.program_id(1) == 0)
    def _():
        carry_ref[...] = jnp.zeros_like(carry_ref)

    x = x_ref[0]
    m = mod_ref[0]
    h = (_rms(x, gpre_ref[...]) * (1.0 + m[shift_row + 1:shift_row + 2]) + m[shift_row:shift_row + 1]).astype(BF16)
    for c in range(wa_ref.shape[1] // fc):
        sl = slice(c * fc, (c + 1) * fc)
        pa = jnp.dot(h, wa_ref[:, sl], preferred_element_type=F32)
        pb = jnp.dot(h, wb_ref[:, sl], preferred_element_type=F32)
        cin = pa if ffn else pb * jnp.dot(h, wc_ref[:, sl], preferred_element_type=F32)
        prev = carry_ref[c]
        carry_ref[c] = cin[tm - SUBLANES:tm]
        w = cw_ref[:, sl]
        z = w[2:3] * cin + w[1:2] * _shift_rows(cin, prev, 1) + w[0:1] * _shift_rows(cin, prev, 2) + cb_ref[:, sl]
        act = (z * _sigmoid(z) * pb) if ffn else (pa * z)
        contrib = jnp.dot(act.astype(BF16), wd_ref[sl, :], preferred_element_type=F32)
        if c == 0:
            acc_ref[...] = contrib
        else:
            acc_ref[...] += contrib
    o_ref[0] = x + (1.0 + m[shift_row + 2:shift_row + 3]) * _rms(acc_ref[...], gpost_ref[...])


def _gated_conv_block(x, mod, gpre, gpost, w_ins, conv_w, conv_b, w_down, *, ffn):
    bsz, t, d = x.shape
    f = w_down.shape[0]
    tm = min(ROW_TILE, t)
    fc = FF_CHUNK
    in_specs = [pl.BlockSpec((1, tm, d), lambda b, i: (b, i, 0)),
                pl.BlockSpec((1, 6, d), lambda b, i: (b, 0, 0)),
                _resident((1, d)), _resident((1, d))]
    in_specs += [_resident((d, f)) for _ in w_ins]
    in_specs += [_resident(conv_w.shape), _resident((1, f)), _resident((f, d))]
    return pl.pallas_call(
        functools.partial(_gcm_kernel, ffn=ffn, fc=fc, shift_row=3 if ffn else 0),
        grid=(bsz, t // tm),
        in_specs=in_specs,
        out_specs=pl.BlockSpec((1, tm, d), lambda b, i: (b, i, 0)),
        out_shape=jax.ShapeDtypeStruct((bsz, t, d), F32),
        scratch_shapes=[pltpu.VMEM((f // fc, SUBLANES, fc), F32), pltpu.VMEM((tm, d), F32)],
        compiler_params=_params("arbitrary", "arbitrary"),
        name="ffn" if ffn else "short_conv_mixer",
    )(x, mod, gpre, gpost, *w_ins, conv_w, conv_b, w_down)


def _rglru_kernel(xr_ref, gr_ref, cw_ref, cb_ref, wri_ref, br_ref, bi_ref, lam_ref, o_ref,
                  xprev_ref, hprev_ref, a_ref, u_ref):
    tt, width = a_ref.shape

    @pl.when(pl.program_id(1) == 0)
    def _():
        xprev_ref[...] = jnp.zeros_like(xprev_ref)
        hprev_ref[...] = jnp.zeros_like(hprev_ref)

    xr = xr_ref[0].astype(F32)
    prev = xprev_ref[...]
    xprev_ref[...] = xr[tt - SUBLANES:tt]
    cw = cw_ref[...]
    xc = (cw[3:4] * xr + cw[2:3] * _shift_rows(xr, prev, 1) + cw[1:2] * _shift_rows(xr, prev, 2)
          + cw[0:1] * _shift_rows(xr, prev, 3) + cb_ref[...])
    neg_lam = -lam_ref[...]
    softplus = jnp.maximum(neg_lam, 0.0) + jnp.log1p(jnp.exp(-jnp.abs(neg_lam)))
    for j in range(width // LANES):
        sl = slice(j * LANES, (j + 1) * LANES)
        xs = xc[:, sl]
        ri = jnp.dot(xs.astype(BF16), wri_ref[j], preferred_element_type=F32)
        r = _sigmoid(ri[:, :LANES] + br_ref[:, sl])
        gate_i = _sigmoid(ri[:, LANES:] + bi_ref[:, sl])
        log_a = -LRU_C * r * softplus[:, sl]
        a_ref[:, sl] = jnp.exp(log_a)
        u_ref[:, sl] = jnp.sqrt(1.0 - jnp.exp(2.0 * log_a)) * (gate_i * xs)

    row = lax.broadcasted_iota(jnp.int32, (SUBLANES, width), 0)

    def body(k, h):
        r0 = pl.multiple_of(k * SUBLANES, SUBLANES)
        a = a_ref[pl.ds(r0, SUBLANES), :]
        b = u_ref[pl.ds(r0, SUBLANES), :]
        for d in (1, 2, 4):
            a_sh = jnp.where(row >= d, pltpu.roll(a, d, axis=0), 1.0)
            b_sh = jnp.where(row >= d, pltpu.roll(b, d, axis=0), 0.0)
            b = a * b_sh + b
            a = a * a_sh
        hh = b + a * h
        u_ref[pl.ds(r0, SUBLANES), :] = hh
        return hh[SUBLANES - 1:SUBLANES]

    hprev_ref[...] = lax.fori_loop(0, tt // SUBLANES, body, hprev_ref[...])
    o_ref[0] = (u_ref[...] * jax.nn.gelu(gr_ref[0].astype(F32))).astype(BF16)


def _rglru(p, conv_w, conv_b, w_ri, b_r, b_i, lam):
    bsz, t, _ = p.shape
    width = conv_w.shape[1]
    tt = min(ROW_TILE, t)
    return pl.pallas_call(
        _rglru_kernel,
        grid=(bsz, t // tt),
        in_specs=[pl.BlockSpec((1, tt, width), lambda b, i: (b, i, 0)),
                  pl.BlockSpec((1, tt, width), lambda b, i: (b, i, 1)),
                  _resident(conv_w.shape), _resident((1, width)), _resident(w_ri.shape),
                  _resident((1, width)), _resident((1, width)), _resident((1, width))],
        out_specs=pl.BlockSpec((1, tt, width), lambda b, i: (b, i, 0)),
        out_shape=jax.ShapeDtypeStruct((bsz, t, width), BF16),
        scratch_shapes=[pltpu.VMEM((SUBLANES, width), F32), pltpu.VMEM((1, width), F32),
                        pltpu.VMEM((tt, width), F32), pltpu.VMEM((tt, width), F32)],
        compiler_params=_params("arbitrary", "arbitrary"),
        name="rglru",
    )(p, p, conv_w, conv_b, w_ri, b_r, b_i, lam)


def _compress_kernel(r_ref, pe_ref, w1_ref, w2_ref, o_ref, *, n_blk):
    tm, half = r_ref.shape
    n_in_group = lax.broadcasted_iota(jnp.int32, (tm, 1), 0) & (n_blk - 1)
    r = r_ref[...].astype(F32)
    pe = pe_ref[0]
    top = jnp.dot((r + pe[0:1]).astype(BF16), w1_ref[0, 0:half], preferred_element_type=F32)
    bot = jnp.dot((r + pe[1:2]).astype(BF16), w1_ref[0, half:], preferred_element_type=F32)
    nxt = pltpu.roll(bot, tm - 1, axis=0)
    hid = top + jnp.where(n_in_group == n_blk - 1, 0.0, nxt)
    o_ref[...] = jnp.dot(jax.nn.gelu(hid).astype(BF16), w2_ref[0], preferred_element_type=F32).astype(BF16)


def _compress(rows_kv, pe, w1, w2, n_blk):
    rows, half = rows_kv.shape
    tm = NSA_KV_HEADS * n_blk
    kv = lambda i: (i % 2, 0, 0)
    return pl.pallas_call(
        functools.partial(_compress_kernel, n_blk=n_blk),
        grid=(rows // tm,),
        in_specs=[pl.BlockSpec((tm, half), lambda i: (i, 0)),
                  pl.BlockSpec((1,) + pe.shape[1:], kv), pl.BlockSpec((1,) + w1.shape[1:], kv),
                  pl.BlockSpec((1,) + w2.shape[1:], kv)],
        out_specs=pl.BlockSpec((tm, HEAD_DIM), lambda i: (i, 0)),
        out_shape=jax.ShapeDtypeStruct((rows, HEAD_DIM), BF16),
        compiler_params=_params("arbitrary"),
        name="nsa_compress",
    )(rows_kv, pe, w1, w2)


LOG2E = 1.4426950408889634
V_ROWS = HEAD_DIM + 2 * SUBLANES


def _rope(x, cos, sin_lo, sin_hi, axis):
    half = ROPE_DIM // 2
    return (x * cos + pltpu.roll(x, half, axis=axis) * sin_hi
            + pltpu.roll(x, x.shape[axis] - half, axis=axis) * sin_lo)


def _online_softmax(s, vt_tile, m_old, acc):
    m_new = jnp.maximum(m_old, jnp.max(s, axis=0, keepdims=True))
    p = jnp.exp2(s - m_new).astype(BF16)
    return m_new, jnp.exp2(m_old - m_new) * acc + jnp.dot(vt_tile, p, preferred_element_type=F32)


def _nsa_kernel(q_ref, kv_ref, kc_ref, vct_ref, cosq_ref, sloq_ref, shiq_ref, cosk_ref, slok_ref, shik_ref, o_ref,
                ks_s, vst_s, kw_s, vwt_s):
    tq = q_ref.shape[1]
    hd, rep = HEAD_DIM, GQA_REP
    n_cmp = kc_ref.shape[2]
    n_slc = ks_s.shape[0] // SLC_BLOCK
    rows = rep * tq
    qi = pl.program_id(2)
    t0 = pl.multiple_of(qi * tq, tq)
    lane = lax.broadcasted_iota(jnp.int32, (1, LANES), 1)
    first = lane < hd

    kv = kv_ref[0]
    cos_k, slo_k, shi_k = cosk_ref[...], slok_ref[...], shik_ref[...]
    key_blk = (t0 + lax.broadcasted_iota(jnp.int32, (tq, LANES), 0)) >> SLC_SHIFT
    blk_onehot = jnp.where(key_blk == lane - hd, 1.0, 0.0)

    def park(pair, extra, k_s, vt_s):
        pair = _rope(pair.astype(F32), cos_k, slo_k, shi_k, 1)
        k_s[pl.ds(t0, tq), :] = jnp.where(first, pair, extra).astype(BF16)
        vt_s[qi] = jnp.where(first, pltpu.roll(pair, hd, axis=1), 1.0).T[0:V_ROWS].astype(BF16)

    park(kv[:, 0:LANES], blk_onehot, ks_s, vst_s)
    park(kv[:, LANES:2 * LANES], 0.0, kw_s, vwt_s)

    q_t = q_ref[0].astype(F32).T * (hd ** -0.5 * LOG2E)
    q_rot_t = _rope(q_t, cosq_ref[...], sloq_ref[...], shiq_ref[...], 0)
    heads_on_lanes = lambda x: jnp.concatenate([x[r * hd:(r + 1) * hd] for r in range(rep)], axis=1)
    t_q = t0 + lax.broadcasted_iota(jnp.int32, (1, tq), 1)
    t_q4 = jnp.concatenate([t_q] * rep, axis=1)

    s = jnp.dot(kc_ref[0, 0], heads_on_lanes(q_t).astype(BF16), preferred_element_type=F32)
    cmp_end = lax.broadcasted_iota(jnp.int32, (n_cmp, rows), 0) * CMP_STRIDE + (CMP_BLOCK - 1)
    mask_c = cmp_end <= t_q4
    s = jnp.where(mask_c, s, NEG)
    e = jnp.where(mask_c, jnp.exp2(s - jnp.max(s, axis=0, keepdims=True)), 0.0)
    den = jnp.sum(e, axis=0, keepdims=True)
    p = e * jnp.where(den > 0.0, 1.0 / den, 0.0)
    o_cmp = jnp.dot(vct_ref[0, 0], p.astype(BF16), preferred_element_type=F32)
    p_sum = p[:, 0:tq]
    for r in range(1, rep):
        p_sum = p_sum + p[:, r * tq:(r + 1) * tq]

    n_pos = lax.broadcasted_iota(jnp.int32, (n_slc, n_cmp), 1) * CMP_STRIDE
    j_pos = lax.broadcasted_iota(jnp.int32, (n_slc, n_cmp), 0) * SLC_BLOCK
    overlap = jnp.where((n_pos < j_pos + SLC_BLOCK) & (n_pos + CMP_BLOCK > j_pos), 1.0, 0.0)
    imp = jnp.dot(overlap, p_sum, preferred_element_type=F32, precision=lax.Precision.HIGHEST)
    blk = lax.broadcasted_iota(jnp.int32, (n_slc, tq), 0)
    imp = jnp.where((blk == 0) | (blk == (t_q >> SLC_SHIFT)), imp + FORCE_BONUS, imp)
    imp = jnp.where(blk * SLC_BLOCK <= t_q, imp, NEG)
    blk_f = blk.astype(F32)
    drop = jnp.full((n_slc, tq), NEG, F32)
    for _ in range(SLC_TOPK):
        mx = jnp.max(imp, axis=0, keepdims=True)
        lowest = jnp.min(jnp.where(imp == mx, blk_f, float(n_slc)), axis=0, keepdims=True)
        pick = blk_f == lowest
        drop = jnp.where(pick, 0.0, drop)
        imp = jnp.where(pick, -jnp.inf, imp)

    w = jnp.concatenate([heads_on_lanes(q_rot_t), jnp.concatenate([drop] * rep, axis=1),
                         jnp.zeros((LANES - hd - n_slc, rows), F32)], axis=0).astype(BF16)
    key_i = lax.broadcasted_iota(jnp.int32, (tq, tq), 0)
    qry_i = lax.broadcasted_iota(jnp.int32, (tq, tq), 1)
    causal = jnp.concatenate([jnp.where(key_i <= qry_i, 0.0, NEG)] * rep, axis=1)
    older = jnp.concatenate([jnp.where(key_i > qry_i, 0.0, NEG)] * rep, axis=1)
    scores = lambda k_s, k0: jnp.dot(k_s[pl.ds(pl.multiple_of(k0, tq), tq), :], w, preferred_element_type=F32)
    m0 = jnp.full((1, rows), NEG, F32)
    acc0 = jnp.zeros((V_ROWS, rows), F32)

    def sel_step(kt, carry):
        s_cur, m, acc = carry
        s_next = scores(ks_s, (kt + 1) * tq)
        m, acc = _online_softmax(s_cur, vst_s[kt], m, acc)
        return s_next, m, acc

    s_diag, m_sel, acc_sel = lax.fori_loop(0, qi, sel_step, (scores(ks_s, 0), m0, acc0))
    _, acc_sel = _online_softmax(s_diag + causal, vst_s[qi], m_sel, acc_sel)

    prev = jnp.maximum(qi - 1, 0)
    m_win, acc_win = _online_softmax(scores(kw_s, prev * tq) + jnp.where(qi > 0, older, NEG), vwt_s[prev], m0, acc0)
    _, acc_win = _online_softmax(scores(kw_s, t0) + causal, vwt_s[qi], m_win, acc_win)

    gate = _sigmoid(kv[:, 2 * LANES:3 * LANES].astype(F32).T[0:2 * SUBLANES])
    outs = []
    for r in range(rep):
        ln = slice(r * tq, (r + 1) * tq)
        g = [gate[3 * r + c:3 * r + c + 1] for c in range(3)]
        outs.append(g[0] * o_cmp[:, ln] + (g[1] / acc_sel[hd:hd + 1, ln]) * acc_sel[0:hd, ln]
                    + (g[2] / acc_win[hd:hd + 1, ln]) * acc_win[0:hd, ln])
    for h in range(rep // 2):
        pair = jnp.concatenate([outs[2 * h], outs[2 * h + 1]], axis=0)
        o_ref[0, :, h * LANES:(h + 1) * LANES] = pair.T.astype(BF16)


def _nsa(p, k_cmp, v_cmp_t, tables_q, tables_k):
    bsz, t, _ = p.shape
    g = NSA_KV_HEADS
    tq = min(Q_TILE, t // 2)
    qw = GQA_REP * HEAD_DIM
    assert WINDOW == tq and t % tq == 0 and SLC_TOPK <= t // SLC_BLOCK <= LANES - HEAD_DIM and 2 * HEAD_DIM == LANES
    tab_q = pl.BlockSpec((qw, tq), lambda b, h, i: (0, i))
    tab_k = pl.BlockSpec((tq, LANES), lambda b, h, i: (i, 0))
    per_group = lambda a: pl.BlockSpec((1, 1) + a.shape[2:], lambda b, h, i: (b, h, 0, 0))
    keys = pltpu.VMEM((t, LANES), BF16)
    vals = pltpu.VMEM((t // tq, V_ROWS, tq), BF16)
    return pl.pallas_call(
        _nsa_kernel,
        grid=(bsz, g, t // tq),
        in_specs=[pl.BlockSpec((1, tq, qw), lambda b, h, i: (b, i, Q_COL0 // qw + h)),
                  pl.BlockSpec((1, tq, GROUP_COLS), lambda b, h, i: (b, i, GROUP_COL0 // GROUP_COLS + h)),
                  per_group(k_cmp), per_group(v_cmp_t), tab_q, tab_q, tab_q, tab_k, tab_k, tab_k],
        out_specs=pl.BlockSpec((1, tq, qw), lambda b, h, i: (b, i, h)),
        out_shape=jax.ShapeDtypeStruct((bsz, t, NSA_HEADS * HEAD_DIM), BF16),
        scratch_shapes=[keys, vals, keys, vals],
        compiler_params=_params("arbitrary", "arbitrary", "arbitrary"),
        name="nsa",
    )(p, p, k_cmp, v_cmp_t, *tables_q, *tables_k)


def _out_kernel(x_ref, ya_ref, yb_ref, mod_ref, gain_ref, wa_ref, wb_ref, o_ref):
    y = (jnp.dot(ya_ref[0], wa_ref[...], preferred_element_type=F32)
         + jnp.dot(yb_ref[0], wb_ref[...], preferred_element_type=F32))
    m = mod_ref[0]
    o_ref[0] = x_ref[0] + (1.0 + m[2:3]) * _rms(y, gain_ref[...])


def _out_proj(x, ya, yb, mod, gain, wa, wb):
    bsz, t, d = x.shape
    tm = min(ROW_TILE, t)
    row = lambda w: pl.BlockSpec((1, tm, w), lambda b, i: (b, i, 0))
    return pl.pallas_call(
        _out_kernel,
        grid=(bsz, t // tm),
        in_specs=[row(d), row(ya.shape[2]), row(yb.shape[2]), pl.BlockSpec((1, 6, d), lambda b, i: (b, 0, 0)),
                  _resident((1, d)), _resident(wa.shape), _resident(wb.shape)],
        out_specs=row(d),
        out_shape=jax.ShapeDtypeStruct((bsz, t, d), F32),
        compiler_params=_params("arbitrary", "arbitrary"),
        name="mixer_out",
    )(x, ya, yb, mod, gain, wa, wb)


def _ab_column_order():
    hd, g = HEAD_DIM, NSA_KV_HEADS
    kv0 = GROUP_COL0
    src = {name: kv0 + n * g * hd for n, name in enumerate(("kc", "vc", "ks", "vs", "kw", "vw"))}
    gates0 = kv0 + 6 * g * hd
    n_gate = 3 * GQA_REP
    cols = list(range(GROUP_COL0))
    for grp in range(g):
        for name in ("ks", "vs", "kw", "vw"):
            cols += list(range(src[name] + grp * hd, src[name] + (grp + 1) * hd))
        cols += list(range(gates0 + grp * n_gate, gates0 + (grp + 1) * n_gate)) + [-1] * (LANES - n_gate)
    cols += list(range(src["kc"], src["kc"] + g * hd)) + list(range(src["vc"], src["vc"] + g * hd))
    return np.asarray(cols, np.int32)


def _rope_tables(t):
    half = ROPE_DIM // 2
    inv = jnp.power(jnp.float32(ROPE_THETA), -jnp.arange(half, dtype=F32) / half)
    ang = jnp.arange(t, dtype=F32)[:, None] * inv[None, :]
    cos, sin = jnp.cos(ang), jnp.sin(ang)
    rest = HEAD_DIM - ROPE_DIM
    one = jnp.ones((t, rest), F32)
    zero = jnp.zeros((t, rest), F32)
    zh = jnp.zeros((t, half), F32)
    head = (jnp.concatenate([cos, cos, one], axis=1),
            jnp.concatenate([-sin, zh, zero], axis=1),
            jnp.concatenate([zh, sin, zero], axis=1))
    ident = (jnp.ones((t, HEAD_DIM), F32), jnp.zeros((t, HEAD_DIM), F32), jnp.zeros((t, HEAD_DIM), F32))
    tables_q = tuple(jnp.tile(a, (1, GQA_REP)).T for a in head)
    tables_k = tuple(jnp.concatenate([a, i], axis=1) for a, i in zip(head, ident))
    return tables_q, tables_k


def _gate_weights(w_r, w_i):
    nb, bw, _ = w_r.shape
    per = LANES // bw
    def slabs(w):
        w = w.reshape(nb // per, per, bw, bw)
        eye = jnp.eye(per, dtype=w.dtype)
        return jnp.einsum("spcd,pq->spcqd", w, eye).reshape(nb // per, LANES, LANES)
    return jnp.concatenate([slabs(w_r), slabs(w_i)], axis=2).astype(BF16)


def kernel(x, c, mod_w, mod_b, norm_mix_pre, norm_mix_post, norm_ffn_pre, norm_ffn_post, ab_w_in, ab_conv_w,
           ab_conv_b, lru_w_r, lru_b_r, lru_w_i, lru_b_i, lru_lam, cmp_pe_k, cmp_pe_v, cmp_wk1, cmp_wk2, cmp_wv1,
           cmp_wv2, ab_w_out, sc_w_in, sc_conv_w, sc_conv_b, sc_w_out, ffn_w_gate, ffn_w_up, ffn_conv_w,
           ffn_conv_b, ffn_w_down):
    bsz, t, d = x.shape
    depth = mod_w.shape[0]
    g, hd = NSA_KV_HEADS, HEAD_DIM
    n_blk = t // CMP_STRIDE
    row2 = lambda a: a.reshape(1, -1)

    mod_all = _modulation(c, mod_w, mod_b).reshape(depth, bsz, 6, d)
    order = _ab_column_order()
    tables_q, tables_k = _rope_tables(t)
    half_blk = CMP_STRIDE * hd

    for i in range(depth):
        mod = mod_all[i]
        if i % 2 == 0:
            e = i // 2
            w_in = jnp.concatenate([ab_w_in[e], jnp.zeros((d, 1), F32)], axis=1)[:, order].astype(BF16)
            p, cmp_in = _proj(x, mod, row2(norm_mix_pre[i]), w_in)
            y_rnn = _rglru(p, ab_conv_w[e], row2(ab_conv_b[e]), _gate_weights(lru_w_r[e], lru_w_i[e]),
                           row2(lru_b_r[e]), row2(lru_b_i[e]), row2(lru_lam[e]))
            stack2 = lambda a, b: jnp.stack([a, b])
            kv_cmp = _compress(cmp_in.reshape(bsz * 2 * g * n_blk, half_blk),
                               stack2(cmp_pe_k[e], cmp_pe_v[e]).reshape(2, 2, half_blk),
                               stack2(cmp_wk1[e], cmp_wv1[e]).astype(BF16),
                               stack2(cmp_wk2[e], cmp_wv2[e]).astype(BF16), n_blk)
            kv_cmp = kv_cmp.reshape(bsz, 2, g, n_blk, hd)
            y_att = _nsa(p, kv_cmp[:, 0], kv_cmp[:, 1].transpose(0, 1, 3, 2), tables_q, tables_k)
            w_out = ab_w_out[e].astype(BF16)
            x = _out_proj(x, y_rnn, y_att, mod, row2(norm_mix_post[i]), w_out[:y_rnn.shape[2]], w_out[y_rnn.shape[2]:])
        else:
            o = i // 2
            w_in = sc_w_in[o].astype(BF16)
            f = sc_w_out.shape[1]
            x = _gated_conv_block(x, mod, row2(norm_mix_pre[i]), row2(norm_mix_post[i]),
                                  [w_in[:, :f], w_in[:, f:2 * f], w_in[:, 2 * f:]],
                                  sc_conv_w[o], row2(sc_conv_b[o]), sc_w_out[o].astype(BF16), ffn=False)
        x = _gated_conv_block(x, mod, row2(norm_ffn_pre[i]), row2(norm_ffn_post[i]),
                              [ffn_w_gate[i].astype(BF16), ffn_w_up[i].astype(BF16)],
                              ffn_conv_w[i], row2(ffn_conv_b[i]), ffn_w_down[i].astype(BF16), ffn=True)
    return x
```

```python
import functools

import numpy as np
import jax
import jax.numpy as jnp
from jax import lax
from jax.experimental import pallas as pl
from jax.experimental.pallas import tpu as pltpu

F32 = jnp.float32
BF16 = jnp.bfloat16

RNN_BLOCKS = 16
LRU_C = 8.0
NSA_HEADS = 16
NSA_KV_HEADS = 4
HEAD_DIM = 64
GQA_REP = NSA_HEADS // NSA_KV_HEADS
CMP_BLOCK = 32
CMP_STRIDE = 16
SLC_BLOCK = 64
SLC_SHIFT = 6
SLC_TOPK = 4
WINDOW = 256
FORCE_BONUS = 1.0e4
ROPE_THETA = 500000.0
ROPE_DIM = HEAD_DIM // 4
EPS = 1e-6
NEG = -1e30

LANES = 128
SUBLANES = 8
VMEM_LIMIT_BYTES = 56 * 1024 * 1024

GROUP_COLS = 384
Q_COL0 = 2048
GROUP_COL0 = 3072
CMP_COL0 = GROUP_COL0 + NSA_KV_HEADS * GROUP_COLS
CMP_COLS = 2 * NSA_KV_HEADS * HEAD_DIM

ROW_TILE = 1024
Q_TILE = 256
FF_CHUNK = 256


def _sigmoid(x):
    return 1.0 / (1.0 + jnp.exp(-x))


def _rms(x, gain):
    return x * lax.rsqrt(jnp.mean(x * x, axis=-1, keepdims=True) + EPS) * gain


def _shift_rows(x, prev8, d):
    r = pltpu.roll(x, d, axis=0)
    p = pltpu.roll(prev8, d, axis=0)
    row = lax.broadcasted_iota(jnp.int32, prev8.shape, 0)
    head = jnp.where(row < d, p, r[0:SUBLANES])
    return jnp.concatenate([head, r[SUBLANES:]], axis=0)


def _params(*semantics):
    return pltpu.CompilerParams(dimension_semantics=semantics, vmem_limit_bytes=VMEM_LIMIT_BYTES)


def _resident(shape):
    nd = len(shape)
    return pl.BlockSpec(shape, lambda *_: (0,) * nd, pipeline_mode=pl.Buffered(1))


def _mod_kernel(c_ref, w_ref, b_ref, o_ref):
    c = c_ref[...]
    act = (c * _sigmoid(c)).astype(BF16)
    o_ref[0] = jnp.dot(act, w_ref[0].astype(BF16), preferred_element_type=F32) + b_ref[0]


def _modulation(c, mod_w, mod_b):
    depth, d, d6 = mod_w.shape
    bsz = c.shape[0]
    n = d6 // d
    return pl.pallas_call(
        _mod_kernel,
        grid=(depth, n),
        in_specs=[pl.BlockSpec((bsz, d), lambda i, j: (0, 0)),
                  pl.BlockSpec((1, d, d), lambda i, j: (i, 0, j)),
                  pl.BlockSpec((1, 1, d), lambda i, j: (i, 0, j))],
        out_specs=pl.BlockSpec((1, bsz, d), lambda i, j: (i, 0, j)),
        out_shape=jax.ShapeDtypeStruct((depth, bsz, d6), F32),
        compiler_params=_params("arbitrary", "arbitrary"),
        name="modulation",
    )(c, mod_w, mod_b.reshape(depth, 1, d6))


def _proj_kernel(x_ref, mod_ref, gain_ref, w_ref, o_ref, oc_ref, *, chunk):
    m = mod_ref[0]
    h = (_rms(x_ref[0], gain_ref[...]) * (1.0 + m[1:2]) + m[0:1]).astype(BF16)
    n_main = o_ref.shape[-1]
    for j in range(n_main // chunk):
        sl = slice(j * chunk, (j + 1) * chunk)
        o_ref[0, :, sl] = jnp.dot(h, w_ref[:, sl], preferred_element_type=F32).astype(BF16)
    cmp = jnp.dot(h, w_ref[:, n_main:], preferred_element_type=F32).astype(BF16)
    for j in range(oc_ref.shape[1]):
        oc_ref[0, j] = cmp[:, j * HEAD_DIM:(j + 1) * HEAD_DIM]


def _proj(x, mod, gain, w):
    bsz, t, d = x.shape
    n = w.shape[1] - CMP_COLS
    tm = min(ROW_TILE, t)
    return pl.pallas_call(
        functools.partial(_proj_kernel, chunk=512),
        grid=(bsz, t // tm),
        in_specs=[pl.BlockSpec((1, tm, d), lambda b, i: (b, i, 0)),
                  pl.BlockSpec((1, 6, d), lambda b, i: (b, 0, 0)),
                  _resident((1, d)),
                  _resident(w.shape)],
        out_specs=[pl.BlockSpec((1, tm, n), lambda b, i: (b, i, 0)),
                   pl.BlockSpec((1, CMP_COLS // HEAD_DIM, tm, HEAD_DIM), lambda b, i: (b, 0, i, 0))],
        out_shape=[jax.ShapeDtypeStruct((bsz, t, n), BF16),
                   jax.ShapeDtypeStruct((bsz, CMP_COLS // HEAD_DIM, t, HEAD_DIM), BF16)],
        compiler_params=_params("arbitrary", "arbitrary"),
        name="proj_in",
    )(x, mod, gain, w)


def _gcm_kernel(*refs, ffn, fc, shift_row):
    if ffn:
        x_ref, mod_ref, gpre_ref, gpost_ref, wa_ref, wb_ref, cw_ref, cb_ref, wd_ref, o_ref, carry_ref, acc_ref = refs
        wc_ref = None
    else:
        (x_ref, mod_ref, gpre_ref, gpost_ref, wa_ref, wb_ref, wc_ref, cw_ref, cb_ref, wd_ref, o_ref,
         carry_ref, acc_ref) = refs
    tm = x_ref.shape[1]

    @pl.when(pl.program_id(1) == 0)
    def _():
        carry_ref[...] = jnp.zeros_like(carry_ref)

    x = x_ref[0]
    m = mod_ref[0]
    h = (_rms(x, gpre_ref[...]) * (1.0 + m[shift_row + 1:shift_row + 2]) + m[shift_row:shift_row + 1]).astype(BF16)
    def project(c):
        sl = slice(c * fc, (c + 1) * fc)
        pa = jnp.dot(h, wa_ref[:, sl], preferred_element_type=F32)
        pb = jnp.dot(h, wb_ref[:, sl], preferred_element_type=F32)
        return pa, pb, (None if ffn else jnp.dot(h, wc_ref[:, sl], preferred_element_type=F32))

    n_chunks = wa_ref.shape[1] // fc
    nxt = project(0)
    for c in range(n_chunks):
        sl = slice(c * fc, (c + 1) * fc)
        pa, pb, pc = nxt
        if c + 1 < n_chunks:
            nxt = project(c + 1)
        cin = pa if ffn else pb * pc
        prev = carry_ref[c]
        carry_ref[c] = cin[tm - SUBLANES:tm]
        w = cw_ref[:, sl]
        z = w[2:3] * cin + w[1:2] * _shift_rows(cin, prev, 1) + w[0:1] * _shift_rows(cin, prev, 2) + cb_ref[:, sl]
        act = (z * _sigmoid(z) * pb) if ffn else (pa * z)
        contrib = jnp.dot(act.astype(BF16), wd_ref[sl, :], preferred_element_type=F32)
        if c == 0:
            acc_ref[...] = contrib
        else:
            acc_ref[...] += contrib
    o_ref[0] = x + (1.0 + m[shift_row + 2:shift_row + 3]) * _rms(acc_ref[...], gpost_ref[...])


def _gated_conv_block(x, mod, gpre, gpost, w_ins, conv_w, conv_b, w_down, *, ffn):
    bsz, t, d = x.shape
    f = w_down.shape[0]
    tm = min(ROW_TILE, t)
    fc = FF_CHUNK
    in_specs = [pl.BlockSpec((1, tm, d), lambda b, i: (b, i, 0)),
                pl.BlockSpec((1, 6, d), lambda b, i: (b, 0, 0)),
                _resident((1, d)), _resident((1, d))]
    in_specs += [_resident((d, f)) for _ in w_ins]
    in_specs += [_resident(conv_w.shape), _resident((1, f)), _resident((f, d))]
    return pl.pallas_call(
        functools.partial(_gcm_kernel, ffn=ffn, fc=fc, shift_row=3 if ffn else 0),
        grid=(bsz, t // tm),
        in_specs=in_specs,
        out_specs=pl.BlockSpec((1, tm, d), lambda b, i: (b, i, 0)),
        out_shape=jax.ShapeDtypeStruct((bsz, t, d), F32),
        scratch_shapes=[pltpu.VMEM((f // fc, SUBLANES, fc), F32), pltpu.VMEM((tm, d), F32)],
        compiler_params=_params("arbitrary", "arbitrary"),
        name="ffn" if ffn else "short_conv_mixer",
    )(x, mod, gpre, gpost, *w_ins, conv_w, conv_b, w_down)


def _rglru_kernel(xr_ref, gr_ref, cw_ref, cb_ref, wri_ref, br_ref, bi_ref, lam_ref, o_ref,
                  xprev_ref, hprev_ref, a_ref, u_ref):
    tt, width = a_ref.shape

    @pl.when(pl.program_id(1) == 0)
    def _():
        xprev_ref[...] = jnp.zeros_like(xprev_ref)
        hprev_ref[...] = jnp.zeros_like(hprev_ref)

    xr = xr_ref[0].astype(F32)
    prev = xprev_ref[...]
    xprev_ref[...] = xr[tt - SUBLANES:tt]
    cw = cw_ref[...]
    xc = (cw[3:4] * xr + cw[2:3] * _shift_rows(xr, prev, 1) + cw[1:2] * _shift_rows(xr, prev, 2)
          + cw[0:1] * _shift_rows(xr, prev, 3) + cb_ref[...])
    neg_lam = -lam_ref[...]
    softplus = jnp.maximum(neg_lam, 0.0) + jnp.log1p(jnp.exp(-jnp.abs(neg_lam)))
    for j in range(width // LANES):
        sl = slice(j * LANES, (j + 1) * LANES)
        xs = xc[:, sl]
        ri = jnp.dot(xs.astype(BF16), wri_ref[j], preferred_element_type=F32)
        r = _sigmoid(ri[:, :LANES] + br_ref[:, sl])
        gate_i = _sigmoid(ri[:, LANES:] + bi_ref[:, sl])
        log_a = -LRU_C * r * softplus[:, sl]
        a = jnp.exp(log_a)
        a_ref[:, sl] = a
        u_ref[:, sl] = jnp.sqrt(1.0 - a * a) * (gate_i * xs)

    row = lax.broadcasted_iota(jnp.int32, (SUBLANES, width), 0)

    def body(k, h):
        r0 = pl.multiple_of(k * SUBLANES, SUBLANES)
        a = a_ref[pl.ds(r0, SUBLANES), :]
        b = u_ref[pl.ds(r0, SUBLANES), :]
        for d in (1, 2, 4):
            a_sh = jnp.where(row >= d, pltpu.roll(a, d, axis=0), 1.0)
            b_sh = jnp.where(row >= d, pltpu.roll(b, d, axis=0), 0.0)
            b = a * b_sh + b
            a = a * a_sh
        hh = b + a * h
        u_ref[pl.ds(r0, SUBLANES), :] = hh
        return hh[SUBLANES - 1:SUBLANES]

    hprev_ref[...] = lax.fori_loop(0, tt // SUBLANES, body, hprev_ref[...])
    o_ref[0] = (u_ref[...] * jax.nn.gelu(gr_ref[0].astype(F32))).astype(BF16)


def _rglru(p, conv_w, conv_b, w_ri, b_r, b_i, lam):
    bsz, t, _ = p.shape
    width = conv_w.shape[1]
    tt = min(ROW_TILE, t)
    return pl.pallas_call(
        _rglru_kernel,
        grid=(bsz, t // tt),
        in_specs=[pl.BlockSpec((1, tt, width), lambda b, i: (b, i, 0)),
                  pl.BlockSpec((1, tt, width), lambda b, i: (b, i, 1)),
                  _resident(conv_w.shape), _resident((1, width)), _resident(w_ri.shape),
                  _resident((1, width)), _resident((1, width)), _resident((1, width))],
        out_specs=pl.BlockSpec((1, tt, width), lambda b, i: (b, i, 0)),
        out_shape=jax.ShapeDtypeStruct((bsz, t, width), BF16),
        scratch_shapes=[pltpu.VMEM((SUBLANES, width), F32), pltpu.VMEM((1, width), F32),
                        pltpu.VMEM((tt, width), F32), pltpu.VMEM((tt, width), F32)],
        compiler_params=_params("arbitrary", "arbitrary"),
        name="rglru",
    )(p, p, conv_w, conv_b, w_ri, b_r, b_i, lam)


def _compress_kernel(r_ref, pe_ref, w1_ref, w2_ref, o_ref, *, n_blk):
    tm, half = r_ref.shape
    n_in_group = lax.broadcasted_iota(jnp.int32, (tm, 1), 0) & (n_blk - 1)
    r = r_ref[...].astype(F32)
    pe = pe_ref[0]
    top = jnp.dot((r + pe[0:1]).astype(BF16), w1_ref[0, 0:half], preferred_element_type=F32)
    bot = jnp.dot((r + pe[1:2]).astype(BF16), w1_ref[0, half:], preferred_element_type=F32)
    nxt = pltpu.roll(bot, tm - 1, axis=0)
    hid = top + jnp.where(n_in_group == n_blk - 1, 0.0, nxt)
    o_ref[...] = jnp.dot(jax.nn.gelu(hid).astype(BF16), w2_ref[0], preferred_element_type=F32).astype(BF16)


def _compress(rows_kv, pe, w1, w2, n_blk):
    rows, half = rows_kv.shape
    tm = NSA_KV_HEADS * n_blk
    kv = lambda i: (i % 2, 0, 0)
    return pl.pallas_call(
        functools.partial(_compress_kernel, n_blk=n_blk),
        grid=(rows // tm,),
        in_specs=[pl.BlockSpec((tm, half), lambda i: (i, 0)),
                  pl.BlockSpec((1,) + pe.shape[1:], kv), pl.BlockSpec((1,) + w1.shape[1:], kv),
                  pl.BlockSpec((1,) + w2.shape[1:], kv)],
        out_specs=pl.BlockSpec((tm, HEAD_DIM), lambda i: (i, 0)),
        out_shape=jax.ShapeDtypeStruct((rows, HEAD_DIM), BF16),
        compiler_params=_params("arbitrary"),
        name="nsa_compress",
    )(rows_kv, pe, w1, w2)


LOG2E = 1.4426950408889634
V_ROWS = HEAD_DIM + 2 * SUBLANES


def _rope(x, cos, sin_lo, sin_hi, axis):
    half = ROPE_DIM // 2
    return (x * cos + pltpu.roll(x, half, axis=axis) * sin_hi
            + pltpu.roll(x, x.shape[axis] - half, axis=axis) * sin_lo)


def _online_softmax(s, vt_tile, m_old, acc):
    m_new = jnp.maximum(m_old, jnp.max(s, axis=0, keepdims=True))
    p = jnp.exp2(s - m_new).astype(BF16)
    return m_new, jnp.exp2(m_old - m_new) * acc + jnp.dot(vt_tile, p, preferred_element_type=F32)


def _nsa_kernel(q_ref, kv_ref, kc_ref, vct_ref, cosq_ref, sloq_ref, shiq_ref, cosk_ref, slok_ref, shik_ref, o_ref,
                ks_s, vst_s, kw_s, vwt_s, s_buf, p_buf, m_buf, a_buf, acc_buf):
    tq = q_ref.shape[1]
    hd, rep = HEAD_DIM, GQA_REP
    n_cmp = kc_ref.shape[2]
    n_slc = ks_s.shape[0] // SLC_BLOCK
    rows = rep * tq
    qi = pl.program_id(2)
    t0 = pl.multiple_of(qi * tq, tq)
    lane = lax.broadcasted_iota(jnp.int32, (1, LANES), 1)
    first = lane < hd

    kv = kv_ref[0]
    cos_k, slo_k, shi_k = cosk_ref[...], slok_ref[...], shik_ref[...]
    key_blk = (t0 + lax.broadcasted_iota(jnp.int32, (tq, LANES), 0)) >> SLC_SHIFT
    blk_onehot = jnp.where(key_blk == lane - hd, 1.0, 0.0)

    def park(pair, extra, k_s, vt_s):
        pair = _rope(pair.astype(F32), cos_k, slo_k, shi_k, 1)
        k_s[pl.ds(t0, tq), :] = jnp.where(first, pair, extra).astype(BF16)
        vt_s[qi] = jnp.where(first, pltpu.roll(pair, hd, axis=1), 1.0).T[0:V_ROWS].astype(BF16)

    park(kv[:, 0:LANES], blk_onehot, ks_s, vst_s)
    park(kv[:, LANES:2 * LANES], 0.0, kw_s, vwt_s)

    q_t = q_ref[0].astype(F32).T * (hd ** -0.5 * LOG2E)
    q_rot_t = _rope(q_t, cosq_ref[...], sloq_ref[...], shiq_ref[...], 0)
    heads_on_lanes = lambda x: jnp.concatenate([x[r * hd:(r + 1) * hd] for r in range(rep)], axis=1)
    t_q = t0 + lax.broadcasted_iota(jnp.int32, (1, tq), 1)
    t_q4 = jnp.concatenate([t_q] * rep, axis=1)
    gate = _sigmoid(kv[:, 2 * LANES:3 * LANES].astype(F32).T[0:2 * SUBLANES])

    w_win = jnp.concatenate([heads_on_lanes(q_rot_t), jnp.zeros((LANES - hd, rows), F32)], axis=0).astype(BF16)
    key_i = lax.broadcasted_iota(jnp.int32, (tq, tq), 0)
    qry_i = lax.broadcasted_iota(jnp.int32, (tq, tq), 1)
    causal = jnp.concatenate([jnp.where(key_i <= qry_i, 0.0, NEG)] * rep, axis=1)
    older = jnp.concatenate([jnp.where(key_i > qry_i, 0.0, NEG)] * rep, axis=1)
    scores = lambda k_s, k0, wq: jnp.dot(k_s[pl.ds(pl.multiple_of(k0, tq), tq), :], wq, preferred_element_type=F32)
    m0 = jnp.full((1, rows), NEG, F32)
    acc0 = jnp.zeros((V_ROWS, rows), F32)

    prev = jnp.maximum(qi - 1, 0)
    s_old = scores(kw_s, prev * tq, w_win)
    s_new = scores(kw_s, t0, w_win)
    s = jnp.dot(kc_ref[0, 0], heads_on_lanes(q_t).astype(BF16), preferred_element_type=F32)

    cmp_end = lax.broadcasted_iota(jnp.int32, (n_cmp, rows), 0) * CMP_STRIDE + (CMP_BLOCK - 1)
    mask_c = cmp_end <= t_q4
    s = jnp.where(mask_c, s, NEG)
    e = jnp.where(mask_c, jnp.exp2(s - jnp.max(s, axis=0, keepdims=True)), 0.0)
    den = jnp.sum(e, axis=0, keepdims=True)
    p = e * jnp.where(den > 0.0, 1.0 / den, 0.0)
    o_cmp = jnp.dot(vct_ref[0, 0], p.astype(BF16), preferred_element_type=F32)
    p_sum = p[:, 0:tq]
    for r in range(1, rep):
        p_sum = p_sum + p[:, r * tq:(r + 1) * tq]

    n_pos = lax.broadcasted_iota(jnp.int32, (n_slc, n_cmp), 1) * CMP_STRIDE
    j_pos = lax.broadcasted_iota(jnp.int32, (n_slc, n_cmp), 0) * SLC_BLOCK
    overlap = jnp.where((n_pos < j_pos + SLC_BLOCK) & (n_pos + CMP_BLOCK > j_pos), 1.0, 0.0)
    imp = jnp.dot(overlap, p_sum, preferred_element_type=F32, precision=lax.Precision.HIGHEST)
    blk = lax.broadcasted_iota(jnp.int32, (n_slc, tq), 0)
    imp = jnp.where((blk == 0) | (blk == (t_q >> SLC_SHIFT)), imp + FORCE_BONUS, imp)
    imp = jnp.where(blk * SLC_BLOCK <= t_q, imp, NEG)
    blk_f = blk.astype(F32)
    drop = jnp.full((n_slc, tq), NEG, F32)
    for _ in range(SLC_TOPK):
        mx = jnp.max(imp, axis=0, keepdims=True)
        lowest = jnp.min(jnp.where(imp == mx, blk_f, float(n_slc)), axis=0, keepdims=True)
        pick = blk_f == lowest
        drop = jnp.where(pick, 0.0, drop)
        imp = jnp.where(pick, -jnp.inf, imp)

    w = jnp.concatenate([heads_on_lanes(q_rot_t), jnp.concatenate([drop] * rep, axis=1),
                         jnp.zeros((LANES - hd - n_slc, rows), F32)], axis=0).astype(BF16)

    m_win, acc_win = _online_softmax(s_old + jnp.where(qi > 0, older, NEG), vwt_s[prev], m0, acc0)
    _, acc_win = _online_softmax(s_new + causal, vwt_s[qi], m_win, acc_win)

    def softmax_stage(s, m_old):
        m_new = jnp.maximum(m_old, jnp.max(s, axis=0, keepdims=True))
        return jnp.exp2(s - m_new).astype(BF16), jnp.exp2(m_old - m_new), m_new

    def value_stage(kt):
        acc_buf[...] = a_buf[...] * acc_buf[...] + jnp.dot(vst_s[kt], p_buf[...], preferred_element_type=F32)

    def sel_step(kt, carry):
        s_next = scores(ks_s, (kt + 1) * tq, w)
        value_stage(jnp.maximum(kt - 1, 0))
        p_buf[...], a_buf[...], m_buf[...] = softmax_stage(s_buf[...], m_buf[...])
        s_buf[...] = s_next
        return carry

    s_buf[...] = scores(ks_s, 0, w)
    m_buf[...] = m0
    acc_buf[...] = acc0
    a_buf[...] = jnp.ones(a_buf.shape, F32)
    p_buf[...] = jnp.zeros(p_buf.shape, BF16)
    lax.fori_loop(0, qi, sel_step, 0)
    value_stage(jnp.maximum(qi - 1, 0))
    p_buf[...], a_buf[...], _ = softmax_stage(s_buf[...] + causal, m_buf[...])
    value_stage(qi)
    acc_sel = acc_buf[...]

    outs = []
    for r in range(rep):
        ln = slice(r * tq, (r + 1) * tq)
        g = [gate[3 * r + c:3 * r + c + 1] for c in range(3)]
        outs.append(g[0] * o_cmp[:, ln] + (g[1] / acc_sel[hd:hd + 1, ln]) * acc_sel[0:hd, ln]
                    + (g[2] / acc_win[hd:hd + 1, ln]) * acc_win[0:hd, ln])
    for h in range(rep // 2):
        pair = jnp.concatenate([outs[2 * h], outs[2 * h + 1]], axis=0)
        o_ref[0, :, h * LANES:(h + 1) * LANES] = pair.T.astype(BF16)


def _nsa(p, k_cmp, v_cmp_t, tables_q, tables_k):
    bsz, t, _ = p.shape
    g = NSA_KV_HEADS
    tq = min(Q_TILE, t // 2)
    qw = GQA_REP * HEAD_DIM
    assert WINDOW == tq and t % tq == 0 and SLC_TOPK <= t // SLC_BLOCK <= LANES - HEAD_DIM and 2 * HEAD_DIM == LANES
    tab_q = pl.BlockSpec((qw, tq), lambda b, h, i: (0, i))
    tab_k = pl.BlockSpec((tq, LANES), lambda b, h, i: (i, 0))
    per_group = lambda a: pl.BlockSpec((1, 1) + a.shape[2:], lambda b, h, i: (b, h, 0, 0))
    keys = pltpu.VMEM((t, LANES), BF16)
    vals = pltpu.VMEM((t // tq, V_ROWS, tq), BF16)
    return pl.pallas_call(
        _nsa_kernel,
        grid=(bsz, g, t // tq),
        in_specs=[pl.BlockSpec((1, tq, qw), lambda b, h, i: (b, i, Q_COL0 // qw + h)),
                  pl.BlockSpec((1, tq, GROUP_COLS), lambda b, h, i: (b, i, GROUP_COL0 // GROUP_COLS + h)),
                  per_group(k_cmp), per_group(v_cmp_t), tab_q, tab_q, tab_q, tab_k, tab_k, tab_k],
        out_specs=pl.BlockSpec((1, tq, qw), lambda b, h, i: (b, i, h)),
        out_shape=jax.ShapeDtypeStruct((bsz, t, NSA_HEADS * HEAD_DIM), BF16),
        scratch_shapes=[keys, vals, keys, vals,
                        pltpu.VMEM((tq, GQA_REP * tq), F32), pltpu.VMEM((tq, GQA_REP * tq), BF16),
                        pltpu.VMEM((1, GQA_REP * tq), F32), pltpu.VMEM((1, GQA_REP * tq), F32),
                        pltpu.VMEM((V_ROWS, GQA_REP * tq), F32)],
        compiler_params=_params("arbitrary", "arbitrary", "arbitrary"),
        name="nsa",
    )(p, p, k_cmp, v_cmp_t, *tables_q, *tables_k)


def _out_kernel(x_ref, ya_ref, yb_ref, mod_ref, gain_ref, wa_ref, wb_ref, o_ref):
    y = (jnp.dot(ya_ref[0], wa_ref[...], preferred_element_type=F32)
         + jnp.dot(yb_ref[0], wb_ref[...], preferred_element_type=F32))
    m = mod_ref[0]
    o_ref[0] = x_ref[0] + (1.0 + m[2:3]) * _rms(y, gain_ref[...])


def _out_proj(x, ya, yb, mod, gain, wa, wb):
    bsz, t, d = x.shape
    tm = min(ROW_TILE, t)
    row = lambda w: pl.BlockSpec((1, tm, w), lambda b, i: (b, i, 0))
    return pl.pallas_call(
        _out_kernel,
        grid=(bsz, t // tm),
        in_specs=[row(d), row(ya.shape[2]), row(yb.shape[2]), pl.BlockSpec((1, 6, d), lambda b, i: (b, 0, 0)),
                  _resident((1, d)), _resident(wa.shape), _resident(wb.shape)],
        out_specs=row(d),
        out_shape=jax.ShapeDtypeStruct((bsz, t, d), F32),
        compiler_params=_params("arbitrary", "arbitrary"),
        name="mixer_out",
    )(x, ya, yb, mod, gain, wa, wb)


def _ab_column_order():
    hd, g = HEAD_DIM, NSA_KV_HEADS
    kv0 = GROUP_COL0
    src = {name: kv0 + n * g * hd for n, name in enumerate(("kc", "vc", "ks", "vs", "kw", "vw"))}
    gates0 = kv0 + 6 * g * hd
    n_gate = 3 * GQA_REP
    cols = list(range(GROUP_COL0))
    for grp in range(g):
        for name in ("ks", "vs", "kw", "vw"):
            cols += list(range(src[name] + grp * hd, src[name] + (grp + 1) * hd))
        cols += list(range(gates0 + grp * n_gate, gates0 + (grp + 1) * n_gate)) + [-1] * (LANES - n_gate)
    cols += list(range(src["kc"], src["kc"] + g * hd)) + list(range(src["vc"], src["vc"] + g * hd))
    return np.asarray(cols, np.int32)


def _rope_tables(t):
    half = ROPE_DIM // 2
    inv = jnp.power(jnp.float32(ROPE_THETA), -jnp.arange(half, dtype=F32) / half)
    ang = jnp.arange(t, dtype=F32)[:, None] * inv[None, :]
    cos, sin = jnp.cos(ang), jnp.sin(ang)
    rest = HEAD_DIM - ROPE_DIM
    one = jnp.ones((t, rest), F32)
    zero = jnp.zeros((t, rest), F32)
    zh = jnp.zeros((t, half), F32)
    head = (jnp.concatenate([cos, cos, one], axis=1),
            jnp.concatenate([-sin, zh, zero], axis=1),
            jnp.concatenate([zh, sin, zero], axis=1))
    ident = (jnp.ones((t, HEAD_DIM), F32), jnp.zeros((t, HEAD_DIM), F32), jnp.zeros((t, HEAD_DIM), F32))
    tables_q = tuple(jnp.tile(a, (1, GQA_REP)).T for a in head)
    tables_k = tuple(jnp.concatenate([a, i], axis=1) for a, i in zip(head, ident))
    return tables_q, tables_k


def _gate_weights(w_r, w_i):
    nb, bw, _ = w_r.shape
    per = LANES // bw
    def slabs(w):
        w = w.reshape(nb // per, per, bw, bw)
        eye = jnp.eye(per, dtype=w.dtype)
        return jnp.einsum("spcd,pq->spcqd", w, eye).reshape(nb // per, LANES, LANES)
    return jnp.concatenate([slabs(w_r), slabs(w_i)], axis=2).astype(BF16)


def kernel(x, c, mod_w, mod_b, norm_mix_pre, norm_mix_post, norm_ffn_pre, norm_ffn_post, ab_w_in, ab_conv_w,
           ab_conv_b, lru_w_r, lru_b_r, lru_w_i, lru_b_i, lru_lam, cmp_pe_k, cmp_pe_v, cmp_wk1, cmp_wk2, cmp_wv1,
           cmp_wv2, ab_w_out, sc_w_in, sc_conv_w, sc_conv_b, sc_w_out, ffn_w_gate, ffn_w_up, ffn_conv_w,
           ffn_conv_b, ffn_w_down):
    bsz, t, d = x.shape
    depth = mod_w.shape[0]
    g, hd = NSA_KV_HEADS, HEAD_DIM
    n_blk = t // CMP_STRIDE
    row2 = lambda a: a.reshape(1, -1)

    mod_all = _modulation(c, mod_w, mod_b).reshape(depth, bsz, 6, d)
    order = _ab_column_order()
    tables_q, tables_k = _rope_tables(t)
    half_blk = CMP_STRIDE * hd

    for i in range(depth):
        mod = mod_all[i]
        if i % 2 == 0:
            e = i // 2
            w_in = jnp.concatenate([ab_w_in[e], jnp.zeros((d, 1), F32)], axis=1)[:, order].astype(BF16)
            p, cmp_in = _proj(x, mod, row2(norm_mix_pre[i]), w_in)
            y_rnn = _rglru(p, ab_conv_w[e], row2(ab_conv_b[e]), _gate_weights(lru_w_r[e], lru_w_i[e]),
                           row2(lru_b_r[e]), row2(lru_b_i[e]), row2(lru_lam[e]))
            stack2 = lambda a, b: jnp.stack([a, b])
            kv_cmp = _compress(cmp_in.reshape(bsz * 2 * g * n_blk, half_blk),
                               stack2(cmp_pe_k[e], cmp_pe_v[e]).reshape(2, 2, half_blk),
                               stack2(cmp_wk1[e], cmp_wv1[e]).astype(BF16),
                               stack2(cmp_wk2[e], cmp_wv2[e]).astype(BF16), n_blk)
            kv_cmp = kv_cmp.reshape(bsz, 2, g, n_blk, hd)
            y_att = _nsa(p, kv_cmp[:, 0], kv_cmp[:, 1].transpose(0, 1, 3, 2), tables_q, tables_k)
            w_out = ab_w_out[e].astype(BF16)
            x = _out_proj(x, y_rnn, y_att, mod, row2(norm_mix_post[i]), w_out[:y_rnn.shape[2]], w_out[y_rnn.shape[2]:])
        else:
            o = i // 2
            w_in = sc_w_in[o].astype(BF16)
            f = sc_w_out.shape[1]
            x = _gated_conv_block(x, mod, row2(norm_mix_pre[i]), row2(norm_mix_post[i]),
                                  [w_in[:, :f], w_in[:, f:2 * f], w_in[:, 2 * f:]],
                                  sc_conv_w[o], row2(sc_conv_b[o]), sc_w_out[o].astype(BF16), ffn=False)
        x = _gated_conv_block(x, mod, row2(norm_ffn_pre[i]), row2(norm_ffn_post[i]),
                              [ffn_w_gate[i].astype(BF16), ffn_w_up[i].astype(BF16)],
                              ffn_conv_w[i], row2(ffn_conv_b[i]), ffn_w_down[i].astype(BF16), ffn=True)
    return x
```

```python
import functools

import numpy as np
import jax
import jax.numpy as jnp
from jax import lax
from jax.experimental import pallas as pl
from jax.experimental.pallas import tpu as pltpu

F32 = jnp.float32
BF16 = jnp.bfloat16

RNN_BLOCKS = 16
LRU_C = 8.0
NSA_HEADS = 16
NSA_KV_HEADS = 4
HEAD_DIM = 64
GQA_REP = NSA_HEADS // NSA_KV_HEADS
CMP_BLOCK = 32
CMP_STRIDE = 16
SLC_BLOCK = 64
SLC_SHIFT = 6
SLC_TOPK = 4
WINDOW = 256
FORCE_BONUS = 1.0e4
ROPE_THETA = 500000.0
ROPE_DIM = HEAD_DIM // 4
EPS = 1e-6
NEG = -1e30
GELU_C0 = 0.7978845608028654
GELU_C1 = GELU_C0 * 0.044715

LANES = 128
SUBLANES = 8
VMEM_LIMIT_BYTES = 56 * 1024 * 1024

GROUP_COLS = 384
Q_COL0 = 2048
GROUP_COL0 = 3072
CMP_COL0 = GROUP_COL0 + NSA_KV_HEADS * GROUP_COLS
CMP_COLS = 2 * NSA_KV_HEADS * HEAD_DIM

ROW_TILE = 1024
Q_TILE = 256
FF_CHUNK = 256


def _sigmoid(x):
    return 1.0 / (1.0 + jnp.exp(-x))


def _rms(x, gain):
    return x * lax.rsqrt(jnp.mean(x * x, axis=-1, keepdims=True) + EPS) * gain


def _shift_rows(x, prev8, d):
    r = pltpu.roll(x, d, axis=0)
    p = pltpu.roll(prev8, d, axis=0)
    row = lax.broadcasted_iota(jnp.int32, prev8.shape, 0)
    head = jnp.where(row < d, p, r[0:SUBLANES])
    return jnp.concatenate([head, r[SUBLANES:]], axis=0)


def _params(*semantics):
    return pltpu.CompilerParams(dimension_semantics=semantics, vmem_limit_bytes=VMEM_LIMIT_BYTES)


def _resident(shape):
    nd = len(shape)
    return pl.BlockSpec(shape, lambda *_: (0,) * nd, pipeline_mode=pl.Buffered(1))


def _mod_kernel(c_ref, w_ref, b_ref, o_ref):
    c = c_ref[...]
    act = (c * _sigmoid(c)).astype(BF16)
    o_ref[0] = jnp.dot(act, w_ref[0].astype(BF16), preferred_element_type=F32) + b_ref[0]


def _modulation(c, mod_w, mod_b):
    depth, d, d6 = mod_w.shape
    bsz = c.shape[0]
    n = d6 // d
    return pl.pallas_call(
        _mod_kernel,
        grid=(depth, n),
        in_specs=[pl.BlockSpec((bsz, d), lambda i, j: (0, 0)),
                  pl.BlockSpec((1, d, d), lambda i, j: (i, 0, j)),
                  pl.BlockSpec((1, 1, d), lambda i, j: (i, 0, j))],
        out_specs=pl.BlockSpec((1, bsz, d), lambda i, j: (i, 0, j)),
        out_shape=jax.ShapeDtypeStruct((depth, bsz, d6), F32),
        compiler_params=_params("arbitrary", "arbitrary"),
        name="modulation",
    )(c, mod_w, mod_b.reshape(depth, 1, d6))


def _proj_kernel(x_ref, mod_ref, gain_ref, w_ref, o_ref, oc_ref, *, chunk):
    m = mod_ref[0]
    h = (_rms(x_ref[0], gain_ref[...]) * (1.0 + m[1:2]) + m[0:1]).astype(BF16)
    n_main = o_ref.shape[-1]
    for j in range(n_main // chunk):
        sl = slice(j * chunk, (j + 1) * chunk)
        o_ref[0, :, sl] = jnp.dot(h, w_ref[:, sl], preferred_element_type=F32).astype(BF16)
    cmp = jnp.dot(h, w_ref[:, n_main:], preferred_element_type=F32).astype(BF16)
    for j in range(oc_ref.shape[1]):
        oc_ref[0, j] = cmp[:, j * HEAD_DIM:(j + 1) * HEAD_DIM]


def _proj(x, mod, gain, w):
    bsz, t, d = x.shape
    n = w.shape[1] - CMP_COLS
    tm = min(ROW_TILE, t)
    return pl.pallas_call(
        functools.partial(_proj_kernel, chunk=512),
        grid=(bsz, t // tm),
        in_specs=[pl.BlockSpec((1, tm, d), lambda b, i: (b, i, 0)),
                  pl.BlockSpec((1, 6, d), lambda b, i: (b, 0, 0)),
                  _resident((1, d)),
                  _resident(w.shape)],
        out_specs=[pl.BlockSpec((1, tm, n), lambda b, i: (b, i, 0)),
                   pl.BlockSpec((1, CMP_COLS // HEAD_DIM, tm, HEAD_DIM), lambda b, i: (b, 0, i, 0))],
        out_shape=[jax.ShapeDtypeStruct((bsz, t, n), BF16),
                   jax.ShapeDtypeStruct((bsz, CMP_COLS // HEAD_DIM, t, HEAD_DIM), BF16)],
        compiler_params=_params("arbitrary", "arbitrary"),
        name="proj_in",
    )(x, mod, gain, w)


def _gcm_kernel(*refs, ffn, fc, shift_row):
    if ffn:
        x_ref, mod_ref, gpre_ref, gpost_ref, wa_ref, wb_ref, cw_ref, cb_ref, wd_ref, o_ref, carry_ref, acc_ref = refs
        wc_ref = None
    else:
        (x_ref, mod_ref, gpre_ref, gpost_ref, wa_ref, wb_ref, wc_ref, cw_ref, cb_ref, wd_ref, o_ref,
         carry_ref, acc_ref) = refs
    tm = x_ref.shape[1]

    @pl.when(pl.program_id(1) == 0)
    def _():
        carry_ref[...] = jnp.zeros_like(carry_ref)

    x = x_ref[0]
    m = mod_ref[0]
    h = (_rms(x, gpre_ref[...]) * (1.0 + m[shift_row + 1:shift_row + 2]) + m[shift_row:shift_row + 1]).astype(BF16)
    def project(c):
        sl = slice(c * fc, (c + 1) * fc)
        pa = jnp.dot(h, wa_ref[:, sl], preferred_element_type=F32)
        pb = jnp.dot(h, wb_ref[:, sl], preferred_element_type=F32)
        return pa, pb, (None if ffn else jnp.dot(h, wc_ref[:, sl], preferred_element_type=F32))

    n_chunks = wa_ref.shape[1] // fc
    nxt = project(0)
    for c in range(n_chunks):
        sl = slice(c * fc, (c + 1) * fc)
        pa, pb, pc = nxt
        if c + 1 < n_chunks:
            nxt = project(c + 1)
        cin = pa if ffn else pb * pc
        prev = carry_ref[c]
        carry_ref[c] = cin[tm - SUBLANES:tm]
        w = cw_ref[:, sl]
        z = w[2:3] * cin + w[1:2] * _shift_rows(cin, prev, 1) + w[0:1] * _shift_rows(cin, prev, 2) + cb_ref[:, sl]
        act = (z * _sigmoid(z) * pb) if ffn else (pa * z)
        contrib = jnp.dot(act.astype(BF16), wd_ref[sl, :], preferred_element_type=F32)
        if c == 0:
            acc_ref[...] = contrib
        else:
            acc_ref[...] += contrib
    o_ref[0] = x + (1.0 + m[shift_row + 2:shift_row + 3]) * _rms(acc_ref[...], gpost_ref[...])


def _gated_conv_block(x, mod, gpre, gpost, w_ins, conv_w, conv_b, w_down, *, ffn):
    bsz, t, d = x.shape
    f = w_down.shape[0]
    tm = min(ROW_TILE, t)
    fc = FF_CHUNK
    in_specs = [pl.BlockSpec((1, tm, d), lambda b, i: (b, i, 0)),
                pl.BlockSpec((1, 6, d), lambda b, i: (b, 0, 0)),
                _resident((1, d)), _resident((1, d))]
    in_specs += [_resident((d, f)) for _ in w_ins]
    in_specs += [_resident(conv_w.shape), _resident((1, f)), _resident((f, d))]
    return pl.pallas_call(
        functools.partial(_gcm_kernel, ffn=ffn, fc=fc, shift_row=3 if ffn else 0),
        grid=(bsz, t // tm),
        in_specs=in_specs,
        out_specs=pl.BlockSpec((1, tm, d), lambda b, i: (b, i, 0)),
        out_shape=jax.ShapeDtypeStruct((bsz, t, d), F32),
        scratch_shapes=[pltpu.VMEM((f // fc, SUBLANES, fc), F32), pltpu.VMEM((tm, d), F32)],
        compiler_params=_params("arbitrary", "arbitrary"),
        name="ffn" if ffn else "short_conv_mixer",
    )(x, mod, gpre, gpost, *w_ins, conv_w, conv_b, w_down)


def _rglru_kernel(xr_ref, gr_ref, cw_ref, cb_ref, wri_ref, br_ref, bi_ref, lam_ref, o_ref,
                  xprev_ref, hprev_ref, a_ref, u_ref):
    n_slabs, tt, _ = a_ref.shape
    seg = tt // SUBLANES

    @pl.when(pl.program_id(1) == 0)
    def _():
        xprev_ref[...] = jnp.zeros_like(xprev_ref)
        hprev_ref[...] = jnp.zeros_like(hprev_ref)

    xr = xr_ref[0].astype(F32)
    prev = xprev_ref[...]
    xprev_ref[...] = xr[tt - SUBLANES:tt]
    cw = cw_ref[...]
    xc = (cw[3:4] * xr + cw[2:3] * _shift_rows(xr, prev, 1) + cw[1:2] * _shift_rows(xr, prev, 2)
          + cw[0:1] * _shift_rows(xr, prev, 3) + cb_ref[...])
    neg_lam = -lam_ref[...]
    softplus = jnp.maximum(neg_lam, 0.0) + jnp.log1p(jnp.exp(-jnp.abs(neg_lam)))
    for j in range(n_slabs):
        sl = slice(j * LANES, (j + 1) * LANES)
        xs = xc[:, sl]
        ri = jnp.dot(xs.astype(BF16), wri_ref[j], preferred_element_type=F32)
        r = _sigmoid(ri[:, :LANES] + br_ref[:, sl])
        gate_i = _sigmoid(ri[:, LANES:] + bi_ref[:, sl])
        a = jnp.exp(-LRU_C * r * softplus[:, sl])
        one_m_a2 = 1.0 - a * a
        root = jnp.where(one_m_a2 > 0.0, one_m_a2 * lax.rsqrt(one_m_a2), 0.0)
        u = root * (gate_i * xs)
        for g in range(SUBLANES):
            a_ref[j, pl.ds(g, seg, stride=SUBLANES), :] = a[g * seg:(g + 1) * seg]
            u_ref[j, pl.ds(g, seg, stride=SUBLANES), :] = u[g * seg:(g + 1) * seg]

    def body(t, carry):
        h, prod = carry
        r0 = pl.multiple_of(t * SUBLANES, SUBLANES)
        a = a_ref[:, pl.ds(r0, SUBLANES), :]
        h = a * h + u_ref[:, pl.ds(r0, SUBLANES), :]
        prod = a * prod
        u_ref[:, pl.ds(r0, SUBLANES), :] = h
        a_ref[:, pl.ds(r0, SUBLANES), :] = prod
        return h, prod

    shape = (n_slabs, SUBLANES, LANES)
    h_end, p_end = lax.fori_loop(0, seg, body, (jnp.zeros(shape, F32), jnp.ones(shape, F32)))

    cur = hprev_ref[...]
    h_in = []
    for g in range(SUBLANES):
        h_in.append(cur)
        cur = h_end[:, g:g + 1] + p_end[:, g:g + 1] * cur
    hprev_ref[...] = cur

    for j in range(n_slabs):
        sl = slice(j * LANES, (j + 1) * LANES)
        for g in range(SUBLANES):
            rows = slice(g * seg, (g + 1) * seg)
            h = (u_ref[j, pl.ds(g, seg, stride=SUBLANES), :]
                 + a_ref[j, pl.ds(g, seg, stride=SUBLANES), :] * h_in[g][j])
            x = gr_ref[0, rows, sl].astype(F32)
            tanh = jnp.tanh(x * (GELU_C0 + GELU_C1 * (x * x)))
            o_ref[0, rows, sl] = ((0.5 * (h * x)) * (1.0 + tanh)).astype(BF16)


def _rglru(p, conv_w, conv_b, w_ri, b_r, b_i, lam):
    bsz, t, _ = p.shape
    width = conv_w.shape[1]
    tt = min(ROW_TILE, t)
    n_slabs = width // LANES
    return pl.pallas_call(
        _rglru_kernel,
        grid=(bsz, t // tt),
        in_specs=[pl.BlockSpec((1, tt, width), lambda b, i: (b, i, 0)),
                  pl.BlockSpec((1, tt, width), lambda b, i: (b, i, 1)),
                  _resident(conv_w.shape), _resident((1, width)), _resident(w_ri.shape),
                  _resident((1, width)), _resident((1, width)), _resident((1, width))],
        out_specs=pl.BlockSpec((1, tt, width), lambda b, i: (b, i, 0)),
        out_shape=jax.ShapeDtypeStruct((bsz, t, width), BF16),
        scratch_shapes=[pltpu.VMEM((SUBLANES, width), F32), pltpu.VMEM((n_slabs, 1, LANES), F32),
                        pltpu.VMEM((n_slabs, tt, LANES), F32), pltpu.VMEM((n_slabs, tt, LANES), F32)],
        compiler_params=_params("arbitrary", "arbitrary"),
        name="rglru",
    )(p, p, conv_w, conv_b, w_ri, b_r, b_i, lam)


def _compress_kernel(r_ref, pe_ref, w1_ref, w2_ref, o_ref, *, n_blk):
    tm, half = r_ref.shape
    n_in_group = lax.broadcasted_iota(jnp.int32, (tm, 1), 0) & (n_blk - 1)
    r = r_ref[...].astype(F32)
    pe = pe_ref[0]
    top = jnp.dot((r + pe[0:1]).astype(BF16), w1_ref[0, 0:half], preferred_element_type=F32)
    bot = jnp.dot((r + pe[1:2]).astype(BF16), w1_ref[0, half:], preferred_element_type=F32)
    nxt = pltpu.roll(bot, tm - 1, axis=0)
    hid = top + jnp.where(n_in_group == n_blk - 1, 0.0, nxt)
    o_ref[...] = jnp.dot(jax.nn.gelu(hid).astype(BF16), w2_ref[0], preferred_element_type=F32).astype(BF16)


def _compress(rows_kv, pe, w1, w2, n_blk):
    rows, half = rows_kv.shape
    tm = NSA_KV_HEADS * n_blk
    kv = lambda i: (i % 2, 0, 0)
    return pl.pallas_call(
        functools.partial(_compress_kernel, n_blk=n_blk),
        grid=(rows // tm,),
        in_specs=[pl.BlockSpec((tm, half), lambda i: (i, 0)),
                  pl.BlockSpec((1,) + pe.shape[1:], kv), pl.BlockSpec((1,) + w1.shape[1:], kv),
                  pl.BlockSpec((1,) + w2.shape[1:], kv)],
        out_specs=pl.BlockSpec((tm, HEAD_DIM), lambda i: (i, 0)),
        out_shape=jax.ShapeDtypeStruct((rows, HEAD_DIM), BF16),
        compiler_params=_params("arbitrary"),
        name="nsa_compress",
    )(rows_kv, pe, w1, w2)


LOG2E = 1.4426950408889634
V_ROWS = HEAD_DIM + 2 * SUBLANES


def _rope(x, cos, sin_lo, sin_hi, axis):
    half = ROPE_DIM // 2
    return (x * cos + pltpu.roll(x, half, axis=axis) * sin_hi
            + pltpu.roll(x, x.shape[axis] - half, axis=axis) * sin_lo)


def _online_softmax(s, vt_tile, m_old, acc):
    m_new = jnp.maximum(m_old, jnp.max(s, axis=0, keepdims=True))
    p = jnp.exp2(s - m_new).astype(BF16)
    return m_new, jnp.exp2(m_old - m_new) * acc + jnp.dot(vt_tile, p, preferred_element_type=F32)


def _nsa_kernel(q_ref, kv_ref, kc_ref, vct_ref, cosq_ref, sloq_ref, shiq_ref, cosk_ref, slok_ref, shik_ref, o_ref,
                ks_s, vst_s, kw_s, vwt_s, s_buf, p_buf, m_buf, a_buf, acc_buf):
    tq = q_ref.shape[1]
    hd, rep = HEAD_DIM, GQA_REP
    n_cmp = kc_ref.shape[2]
    n_slc = ks_s.shape[0] // SLC_BLOCK
    rows = rep * tq
    qi = pl.program_id(2)
    t0 = pl.multiple_of(qi * tq, tq)
    lane = lax.broadcasted_iota(jnp.int32, (1, LANES), 1)
    first = lane < hd

    kv = kv_ref[0]
    cos_k, slo_k, shi_k = cosk_ref[...], slok_ref[...], shik_ref[...]
    key_blk = (t0 + lax.broadcasted_iota(jnp.int32, (tq, LANES), 0)) >> SLC_SHIFT
    blk_onehot = jnp.where(key_blk == lane - hd, 1.0, 0.0)

    def park(pair, extra, k_s, vt_s):
        pair = _rope(pair.astype(F32), cos_k, slo_k, shi_k, 1)
        k_s[pl.ds(t0, tq), :] = jnp.where(first, pair, extra).astype(BF16)
        vt_s[qi] = jnp.where(first, pltpu.roll(pair, hd, axis=1), 1.0).T[0:V_ROWS].astype(BF16)

    park(kv[:, 0:LANES], blk_onehot, ks_s, vst_s)
    park(kv[:, LANES:2 * LANES], 0.0, kw_s, vwt_s)

    q_t = q_ref[0].astype(F32).T * (hd ** -0.5 * LOG2E)
    q_rot_t = _rope(q_t, cosq_ref[...], sloq_ref[...], shiq_ref[...], 0)
    heads_on_lanes = lambda x: jnp.concatenate([x[r * hd:(r + 1) * hd] for r in range(rep)], axis=1)
    t_q = t0 + lax.broadcasted_iota(jnp.int32, (1, tq), 1)
    t_q4 = jnp.concatenate([t_q] * rep, axis=1)
    gate = _sigmoid(kv[:, 2 * LANES:3 * LANES].astype(F32).T[0:2 * SUBLANES])

    w_win = jnp.concatenate([heads_on_lanes(q_rot_t), jnp.zeros((LANES - hd, rows), F32)], axis=0).astype(BF16)
    key_i = lax.broadcasted_iota(jnp.int32, (tq, tq), 0)
    qry_i = lax.broadcasted_iota(jnp.int32, (tq, tq), 1)
    causal = jnp.concatenate([jnp.where(key_i <= qry_i, 0.0, NEG)] * rep, axis=1)
    older = jnp.concatenate([jnp.where(key_i > qry_i, 0.0, NEG)] * rep, axis=1)
    scores = lambda k_s, k0, wq: jnp.dot(k_s[pl.ds(pl.multiple_of(k0, tq), tq), :], wq, preferred_element_type=F32)
    m0 = jnp.full((1, rows), NEG, F32)
    acc0 = jnp.zeros((V_ROWS, rows), F32)

    prev = jnp.maximum(qi - 1, 0)
    s_old = scores(kw_s, prev * tq, w_win)
    s_new = scores(kw_s, t0, w_win)
    s = jnp.dot(kc_ref[0, 0], heads_on_lanes(q_t).astype(BF16), preferred_element_type=F32)

    cmp_end = lax.broadcasted_iota(jnp.int32, (n_cmp, rows), 0) * CMP_STRIDE + (CMP_BLOCK - 1)
    mask_c = cmp_end <= t_q4
    s = jnp.where(mask_c, s, NEG)
    e = jnp.where(mask_c, jnp.exp2(s - jnp.max(s, axis=0, keepdims=True)), 0.0)
    den = jnp.sum(e, axis=0, keepdims=True)
    p = e * jnp.where(den > 0.0, 1.0 / den, 0.0)
    o_cmp = jnp.dot(vct_ref[0, 0], p.astype(BF16), preferred_element_type=F32)
    p_sum = p[:, 0:tq]
    for r in range(1, rep):
        p_sum = p_sum + p[:, r * tq:(r + 1) * tq]

    n_pos = lax.broadcasted_iota(jnp.int32, (n_slc, n_cmp), 1) * CMP_STRIDE
    j_pos = lax.broadcasted_iota(jnp.int32, (n_slc, n_cmp), 0) * SLC_BLOCK
    overlap = jnp.where((n_pos < j_pos + SLC_BLOCK) & (n_pos + CMP_BLOCK > j_pos), 1.0, 0.0)
    imp = jnp.dot(overlap, p_sum, preferred_element_type=F32, precision=lax.Precision.HIGHEST)
    blk = lax.broadcasted_iota(jnp.int32, (n_slc, tq), 0)
    imp = jnp.where((blk == 0) | (blk == (t_q >> SLC_SHIFT)), imp + FORCE_BONUS, imp)
    imp = jnp.where(blk * SLC_BLOCK <= t_q, imp, NEG)
    blk_f = blk.astype(F32)
    drop = jnp.full((n_slc, tq), NEG, F32)
    for _ in range(SLC_TOPK):
        mx = jnp.max(imp, axis=0, keepdims=True)
        lowest = jnp.min(jnp.where(imp == mx, blk_f, float(n_slc)), axis=0, keepdims=True)
        pick = blk_f == lowest
        drop = jnp.where(pick, 0.0, drop)
        imp = jnp.where(pick, -jnp.inf, imp)

    w = jnp.concatenate([heads_on_lanes(q_rot_t), jnp.concatenate([drop] * rep, axis=1),
                         jnp.zeros((LANES - hd - n_slc, rows), F32)], axis=0).astype(BF16)

    m_win, acc_win = _online_softmax(s_old + jnp.where(qi > 0, older, NEG), vwt_s[prev], m0, acc0)
    _, acc_win = _online_softmax(s_new + causal, vwt_s[qi], m_win, acc_win)

    def softmax_stage(s, m_old):
        m_new = jnp.maximum(m_old, jnp.max(s, axis=0, keepdims=True))
        return jnp.exp2(s - m_new).astype(BF16), jnp.exp2(m_old - m_new), m_new

    def value_stage(kt):
        acc_buf[...] = a_buf[...] * acc_buf[...] + jnp.dot(vst_s[kt], p_buf[...], preferred_element_type=F32)

    def sel_step(kt, carry):
        s_next = scores(ks_s, (kt + 1) * tq, w)
        value_stage(jnp.maximum(kt - 1, 0))
        p_buf[...], a_buf[...], m_buf[...] = softmax_stage(s_buf[...], m_buf[...])
        s_buf[...] = s_next
        return carry

    s_buf[...] = scores(ks_s, 0, w)
    m_buf[...] = m0
    acc_buf[...] = acc0
    a_buf[...] = jnp.ones(a_buf.shape, F32)
    p_buf[...] = jnp.zeros(p_buf.shape, BF16)
    lax.fori_loop(0, qi, sel_step, 0)
    value_stage(jnp.maximum(qi - 1, 0))
    p_buf[...], a_buf[...], _ = softmax_stage(s_buf[...] + causal, m_buf[...])
    value_stage(qi)
    acc_sel = acc_buf[...]

    outs = []
    for r in range(rep):
        ln = slice(r * tq, (r + 1) * tq)
        g = [gate[3 * r + c:3 * r + c + 1] for c in range(3)]
        outs.append(g[0] * o_cmp[:, ln] + (g[1] / acc_sel[hd:hd + 1, ln]) * acc_sel[0:hd, ln]
                    + (g[2] / acc_win[hd:hd + 1, ln]) * acc_win[0:hd, ln])
    for h in range(rep // 2):
        pair = jnp.concatenate([outs[2 * h], outs[2 * h + 1]], axis=0)
        o_ref[0, :, h * LANES:(h + 1) * LANES] = pair.T.astype(BF16)


def _nsa(p, k_cmp, v_cmp_t, tables_q, tables_k):
    bsz, t, _ = p.shape
    g = NSA_KV_HEADS
    tq = min(Q_TILE, t // 2)
    qw = GQA_REP * HEAD_DIM
    assert WINDOW == tq and t % tq == 0 and SLC_TOPK <= t // SLC_BLOCK <= LANES - HEAD_DIM and 2 * HEAD_DIM == LANES
    tab_q = pl.BlockSpec((qw, tq), lambda b, h, i: (0, i))
    tab_k = pl.BlockSpec((tq, LANES), lambda b, h, i: (i, 0))
    per_group = lambda a: pl.BlockSpec((1, 1) + a.shape[2:], lambda b, h, i: (b, h, 0, 0))
    keys = pltpu.VMEM((t, LANES), BF16)
    vals = pltpu.VMEM((t // tq, V_ROWS, tq), BF16)
    return pl.pallas_call(
        _nsa_kernel,
        grid=(bsz, g, t // tq),
        in_specs=[pl.BlockSpec((1, tq, qw), lambda b, h, i: (b, i, Q_COL0 // qw + h)),
                  pl.BlockSpec((1, tq, GROUP_COLS), lambda b, h, i: (b, i, GROUP_COL0 // GROUP_COLS + h)),
                  per_group(k_cmp), per_group(v_cmp_t), tab_q, tab_q, tab_q, tab_k, tab_k, tab_k],
        out_specs=pl.BlockSpec((1, tq, qw), lambda b, h, i: (b, i, h)),
        out_shape=jax.ShapeDtypeStruct((bsz, t, NSA_HEADS * HEAD_DIM), BF16),
        scratch_shapes=[keys, vals, keys, vals,
                        pltpu.VMEM((tq, GQA_REP * tq), F32), pltpu.VMEM((tq, GQA_REP * tq), BF16),
                        pltpu.VMEM((1, GQA_REP * tq), F32), pltpu.VMEM((1, GQA_REP * tq), F32),
                        pltpu.VMEM((V_ROWS, GQA_REP * tq), F32)],
        compiler_params=_params("arbitrary", "arbitrary", "arbitrary"),
        name="nsa",
    )(p, p, k_cmp, v_cmp_t, *tables_q, *tables_k)


def _out_kernel(x_ref, ya_ref, yb_ref, mod_ref, gain_ref, wa_ref, wb_ref, o_ref):
    y = (jnp.dot(ya_ref[0], wa_ref[...], preferred_element_type=F32)
         + jnp.dot(yb_ref[0], wb_ref[...], preferred_element_type=F32))
    m = mod_ref[0]
    o_ref[0] = x_ref[0] + (1.0 + m[2:3]) * _rms(y, gain_ref[...])


def _out_proj(x, ya, yb, mod, gain, wa, wb):
    bsz, t, d = x.shape
    tm = min(ROW_TILE, t)
    row = lambda w: pl.BlockSpec((1, tm, w), lambda b, i: (b, i, 0))
    return pl.pallas_call(
        _out_kernel,
        grid=(bsz, t // tm),
        in_specs=[row(d), row(ya.shape[2]), row(yb.shape[2]), pl.BlockSpec((1, 6, d), lambda b, i: (b, 0, 0)),
                  _resident((1, d)), _resident(wa.shape), _resident(wb.shape)],
        out_specs=row(d),
        out_shape=jax.ShapeDtypeStruct((bsz, t, d), F32),
        compiler_params=_params("arbitrary", "arbitrary"),
        name="mixer_out",
    )(x, ya, yb, mod, gain, wa, wb)


def _ab_column_order():
    hd, g = HEAD_DIM, NSA_KV_HEADS
    kv0 = GROUP_COL0
    src = {name: kv0 + n * g * hd for n, name in enumerate(("kc", "vc", "ks", "vs", "kw", "vw"))}
    gates0 = kv0 + 6 * g * hd
    n_gate = 3 * GQA_REP
    cols = list(range(GROUP_COL0))
    for grp in range(g):
        for name in ("ks", "vs", "kw", "vw"):
            cols += list(range(src[name] + grp * hd, src[name] + (grp + 1) * hd))
        cols += list(range(gates0 + grp * n_gate, gates0 + (grp + 1) * n_gate)) + [-1] * (LANES - n_gate)
    cols += list(range(src["kc"], src["kc"] + g * hd)) + list(range(src["vc"], src["vc"] + g * hd))
    return np.asarray(cols, np.int32)


def _rope_tables(t):
    half = ROPE_DIM // 2
    inv = jnp.power(jnp.float32(ROPE_THETA), -jnp.arange(half, dtype=F32) / half)
    ang = jnp.arange(t, dtype=F32)[:, None] * inv[None, :]
    cos, sin = jnp.cos(ang), jnp.sin(ang)
    rest = HEAD_DIM - ROPE_DIM
    one = jnp.ones((t, rest), F32)
    zero = jnp.zeros((t, rest), F32)
    zh = jnp.zeros((t, half), F32)
    head = (jnp.concatenate([cos, cos, one], axis=1),
            jnp.concatenate([-sin, zh, zero], axis=1),
            jnp.concatenate([zh, sin, zero], axis=1))
    ident = (jnp.ones((t, HEAD_DIM), F32), jnp.zeros((t, HEAD_DIM), F32), jnp.zeros((t, HEAD_DIM), F32))
    tables_q = tuple(jnp.tile(a, (1, GQA_REP)).T for a in head)
    tables_k = tuple(jnp.concatenate([a, i], axis=1) for a, i in zip(head, ident))
    return tables_q, tables_k


def _gate_weights(w_r, w_i):
    nb, bw, _ = w_r.shape
    per = LANES // bw
    def slabs(w):
        w = w.reshape(nb // per, per, bw, bw)
        eye = jnp.eye(per, dtype=w.dtype)
        return jnp.einsum("spcd,pq->spcqd", w, eye).reshape(nb // per, LANES, LANES)
    return jnp.concatenate([slabs(w_r), slabs(w_i)], axis=2).astype(BF16)


def kernel(x, c, mod_w, mod_b, norm_mix_pre, norm_mix_post, norm_ffn_pre, norm_ffn_post, ab_w_in, ab_conv_w,
           ab_conv_b, lru_w_r, lru_b_r, lru_w_i, lru_b_i, lru_lam, cmp_pe_k, cmp_pe_v, cmp_wk1, cmp_wk2, cmp_wv1,
           cmp_wv2, ab_w_out, sc_w_in, sc_conv_w, sc_conv_b, sc_w_out, ffn_w_gate, ffn_w_up, ffn_conv_w,
           ffn_conv_b, ffn_w_down):
    bsz, t, d = x.shape
    depth = mod_w.shape[0]
    g, hd = NSA_KV_HEADS, HEAD_DIM
    n_blk = t // CMP_STRIDE
    row2 = lambda a: a.reshape(1, -1)

    mod_all = _modulation(c, mod_w, mod_b).reshape(depth, bsz, 6, d)
    order = _ab_column_order()
    tables_q, tables_k = _rope_tables(t)
    half_blk = CMP_STRIDE * hd

    for i in range(depth):
        mod = mod_all[i]
        if i % 2 == 0:
            e = i // 2
            w_in = jnp.concatenate([ab_w_in[e], jnp.zeros((d, 1), F32)], axis=1)[:, order].astype(BF16)
            p, cmp_in = _proj(x, mod, row2(norm_mix_pre[i]), w_in)
            y_rnn = _rglru(p, ab_conv_w[e], row2(ab_conv_b[e]), _gate_weights(lru_w_r[e], lru_w_i[e]),
                           row2(lru_b_r[e]), row2(lru_b_i[e]), row2(lru_lam[e]))
            stack2 = lambda a, b: jnp.stack([a, b])
            kv_cmp = _compress(cmp_in.reshape(bsz * 2 * g * n_blk, half_blk),
                               stack2(cmp_pe_k[e], cmp_pe_v[e]).reshape(2, 2, half_blk),
                               stack2(cmp_wk1[e], cmp_wv1[e]).astype(BF16),
                               stack2(cmp_wk2[e], cmp_wv2[e]).astype(BF16), n_blk)
            kv_cmp = kv_cmp.reshape(bsz, 2, g, n_blk, hd)
            y_att = _nsa(p, kv_cmp[:, 0], kv_cmp[:, 1].transpose(0, 1, 3, 2), tables_q, tables_k)
            w_out = ab_w_out[e].astype(BF16)
            x = _out_proj(x, y_rnn, y_att, mod, row2(norm_mix_post[i]), w_out[:y_rnn.shape[2]], w_out[y_rnn.shape[2]:])
        else:
            o = i // 2
            w_in = sc_w_in[o].astype(BF16)
            f = sc_w_out.shape[1]
            x = _gated_conv_block(x, mod, row2(norm_mix_pre[i]), row2(norm_mix_post[i]),
                                  [w_in[:, :f], w_in[:, f:2 * f], w_in[:, 2 * f:]],
                                  sc_conv_w[o], row2(sc_conv_b[o]), sc_w_out[o].astype(BF16), ffn=False)
        x = _gated_conv_block(x, mod, row2(norm_ffn_pre[i]), row2(norm_ffn_post[i]),
                              [ffn_w_gate[i].astype(BF16), ffn_w_up[i].astype(BF16)],
                              ffn_conv_w[i], row2(ffn_conv_b[i]), ffn_w_down[i].astype(BF16), ffn=True)
    return x
```

```python
import functools

import numpy as np
import jax
import jax.numpy as jnp
from jax import lax
from jax.experimental import pallas as pl
from jax.experimental.pallas import tpu as pltpu

F32 = jnp.float32
BF16 = jnp.bfloat16

RNN_BLOCKS = 16
LRU_C = 8.0
NSA_HEADS = 16
NSA_KV_HEADS = 4
HEAD_DIM = 64
GQA_REP = NSA_HEADS // NSA_KV_HEADS
CMP_BLOCK = 32
CMP_STRIDE = 16
SLC_BLOCK = 64
SLC_SHIFT = 6
SLC_TOPK = 4
WINDOW = 256
FORCE_BONUS = 1.0e4
ROPE_THETA = 500000.0
ROPE_DIM = HEAD_DIM // 4
EPS = 1e-6
NEG = -1e30
GELU_C0 = 0.7978845608028654
GELU_C1 = GELU_C0 * 0.044715

LANES = 128
SUBLANES = 8
VMEM_LIMIT_BYTES = 56 * 1024 * 1024

GROUP_COLS = 384
Q_COL0 = 2048
GROUP_COL0 = 3072
CMP_COL0 = GROUP_COL0 + NSA_KV_HEADS * GROUP_COLS
CMP_COLS = 2 * NSA_KV_HEADS * HEAD_DIM

ROW_TILE = 1024
Q_TILE = 256
FF_CHUNK = 256


def _sigmoid(x):
    return 1.0 / (1.0 + jnp.exp(-x))


def _rms(x, gain):
    return x * lax.rsqrt(jnp.mean(x * x, axis=-1, keepdims=True) + EPS) * gain


def _shift_rows(x, prev8, d):
    r = pltpu.roll(x, d, axis=0)
    p = pltpu.roll(prev8, d, axis=0)
    row = lax.broadcasted_iota(jnp.int32, prev8.shape, 0)
    head = jnp.where(row < d, p, r[0:SUBLANES])
    return jnp.concatenate([head, r[SUBLANES:]], axis=0)


def _params(*semantics):
    return pltpu.CompilerParams(dimension_semantics=semantics, vmem_limit_bytes=VMEM_LIMIT_BYTES)


def _resident(shape):
    nd = len(shape)
    return pl.BlockSpec(shape, lambda *_: (0,) * nd, pipeline_mode=pl.Buffered(1))


def _mod_kernel(c_ref, w_ref, b_ref, o_ref):
    c = c_ref[...]
    act = (c * _sigmoid(c)).astype(BF16)
    o_ref[0] = jnp.dot(act, w_ref[0].astype(BF16), preferred_element_type=F32) + b_ref[0]


def _modulation(c, mod_w, mod_b):
    depth, d, d6 = mod_w.shape
    bsz = c.shape[0]
    n = d6 // d
    return pl.pallas_call(
        _mod_kernel,
        grid=(depth, n),
        in_specs=[pl.BlockSpec((bsz, d), lambda i, j: (0, 0)),
                  pl.BlockSpec((1, d, d), lambda i, j: (i, 0, j)),
                  pl.BlockSpec((1, 1, d), lambda i, j: (i, 0, j))],
        out_specs=pl.BlockSpec((1, bsz, d), lambda i, j: (i, 0, j)),
        out_shape=jax.ShapeDtypeStruct((depth, bsz, d6), F32),
        compiler_params=_params("arbitrary", "arbitrary"),
        name="modulation",
    )(c, mod_w, mod_b.reshape(depth, 1, d6))


def _proj_kernel(x_ref, mod_ref, gain_ref, w_ref, o_ref, oc_ref, *, chunk):
    m = mod_ref[0]
    h = (_rms(x_ref[0], gain_ref[...]) * (1.0 + m[1:2]) + m[0:1]).astype(BF16)
    n_main = o_ref.shape[-1]
    for j in range(n_main // chunk):
        sl = slice(j * chunk, (j + 1) * chunk)
        o_ref[0, :, sl] = jnp.dot(h, w_ref[:, sl], preferred_element_type=F32).astype(BF16)
    cmp = jnp.dot(h, w_ref[:, n_main:], preferred_element_type=F32).astype(BF16)
    for j in range(oc_ref.shape[1]):
        oc_ref[0, j] = cmp[:, j * HEAD_DIM:(j + 1) * HEAD_DIM]


def _proj(x, mod, gain, w):
    bsz, t, d = x.shape
    n = w.shape[1] - CMP_COLS
    tm = min(ROW_TILE, t)
    return pl.pallas_call(
        functools.partial(_proj_kernel, chunk=512),
        grid=(bsz, t // tm),
        in_specs=[pl.BlockSpec((1, tm, d), lambda b, i: (b, i, 0)),
                  pl.BlockSpec((1, 6, d), lambda b, i: (b, 0, 0)),
                  _resident((1, d)),
                  _resident(w.shape)],
        out_specs=[pl.BlockSpec((1, tm, n), lambda b, i: (b, i, 0)),
                   pl.BlockSpec((1, CMP_COLS // HEAD_DIM, tm, HEAD_DIM), lambda b, i: (b, 0, i, 0))],
        out_shape=[jax.ShapeDtypeStruct((bsz, t, n), BF16),
                   jax.ShapeDtypeStruct((bsz, CMP_COLS // HEAD_DIM, t, HEAD_DIM), BF16)],
        compiler_params=_params("arbitrary", "arbitrary"),
        name="proj_in",
    )(x, mod, gain, w)


def _gcm_kernel(*refs, ffn, fc, shift_row):
    if ffn:
        x_ref, mod_ref, gpre_ref, gpost_ref, wa_ref, wb_ref, cw_ref, cb_ref, wd_ref, o_ref, carry_ref, acc_ref = refs
        wc_ref = None
    else:
        (x_ref, mod_ref, gpre_ref, gpost_ref, wa_ref, wb_ref, wc_ref, cw_ref, cb_ref, wd_ref, o_ref,
         carry_ref, acc_ref) = refs
    tm = x_ref.shape[1]

    @pl.when(pl.program_id(1) == 0)
    def _():
        carry_ref[...] = jnp.zeros_like(carry_ref)

    x = x_ref[0]
    m = mod_ref[0]
    h = (_rms(x, gpre_ref[...]) * (1.0 + m[shift_row + 1:shift_row + 2]) + m[shift_row:shift_row + 1]).astype(BF16)
    def project(c):
        sl = slice(c * fc, (c + 1) * fc)
        pa = jnp.dot(h, wa_ref[:, sl], preferred_element_type=F32)
        pb = jnp.dot(h, wb_ref[:, sl], preferred_element_type=F32)
        return pa, pb, (None if ffn else jnp.dot(h, wc_ref[:, sl], preferred_element_type=F32))

    n_chunks = wa_ref.shape[1] // fc
    nxt = project(0)
    for c in range(n_chunks):
        sl = slice(c * fc, (c + 1) * fc)
        pa, pb, pc = nxt
        if c + 1 < n_chunks:
            nxt = project(c + 1)
        cin = pa if ffn else pb * pc
        prev = carry_ref[c]
        carry_ref[c] = cin[tm - SUBLANES:tm]
        w = cw_ref[:, sl]
        z = w[2:3] * cin + w[1:2] * _shift_rows(cin, prev, 1) + w[0:1] * _shift_rows(cin, prev, 2) + cb_ref[:, sl]
        act = (z * _sigmoid(z) * pb) if ffn else (pa * z)
        acc_ref[:, sl] = act.astype(BF16)
    y = jnp.dot(acc_ref[...], wd_ref[...], preferred_element_type=F32)
    o_ref[0] = x + (1.0 + m[shift_row + 2:shift_row + 3]) * _rms(y, gpost_ref[...])


def _gated_conv_block(x, mod, gpre, gpost, w_ins, conv_w, conv_b, w_down, *, ffn):
    bsz, t, d = x.shape
    f = w_down.shape[0]
    tm = min(ROW_TILE, t)
    fc = FF_CHUNK
    in_specs = [pl.BlockSpec((1, tm, d), lambda b, i: (b, i, 0)),
                pl.BlockSpec((1, 6, d), lambda b, i: (b, 0, 0)),
                _resident((1, d)), _resident((1, d))]
    in_specs += [_resident((d, f)) for _ in w_ins]
    in_specs += [_resident(conv_w.shape), _resident((1, f)), _resident((f, d))]
    return pl.pallas_call(
        functools.partial(_gcm_kernel, ffn=ffn, fc=fc, shift_row=3 if ffn else 0),
        grid=(bsz, t // tm),
        in_specs=in_specs,
        out_specs=pl.BlockSpec((1, tm, d), lambda b, i: (b, i, 0)),
        out_shape=jax.ShapeDtypeStruct((bsz, t, d), F32),
        scratch_shapes=[pltpu.VMEM((f // fc, SUBLANES, fc), F32), pltpu.VMEM((tm, f), BF16)],
        compiler_params=_params("arbitrary", "arbitrary"),
        name="ffn" if ffn else "short_conv_mixer",
    )(x, mod, gpre, gpost, *w_ins, conv_w, conv_b, w_down)


def _rglru_kernel(xr_ref, gr_ref, cw_ref, cb_ref, wri_ref, br_ref, bi_ref, lam_ref, o_ref,
                  xprev_ref, hprev_ref, a_ref, u_ref):
    n_slabs, tt, _ = a_ref.shape
    seg = tt // SUBLANES

    @pl.when(pl.program_id(1) == 0)
    def _():
        xprev_ref[...] = jnp.zeros_like(xprev_ref)
        hprev_ref[...] = jnp.zeros_like(hprev_ref)

    xr = xr_ref[0].astype(F32)
    prev = xprev_ref[...]
    xprev_ref[...] = xr[tt - SUBLANES:tt]
    cw = cw_ref[...]
    xc = (cw[3:4] * xr + cw[2:3] * _shift_rows(xr, prev, 1) + cw[1:2] * _shift_rows(xr, prev, 2)
          + cw[0:1] * _shift_rows(xr, prev, 3) + cb_ref[...])
    neg_lam = -lam_ref[...]
    softplus = jnp.maximum(neg_lam, 0.0) + jnp.log1p(jnp.exp(-jnp.abs(neg_lam)))
    for j in range(n_slabs):
        sl = slice(j * LANES, (j + 1) * LANES)
        xs = xc[:, sl]
        ri = jnp.dot(xs.astype(BF16), wri_ref[j], preferred_element_type=F32)
        r = _sigmoid(ri[:, :LANES] + br_ref[:, sl])
        gate_i = _sigmoid(ri[:, LANES:] + bi_ref[:, sl])
        a = jnp.exp(-LRU_C * r * softplus[:, sl])
        one_m_a2 = 1.0 - a * a
        root = jnp.where(one_m_a2 > 0.0, one_m_a2 * lax.rsqrt(one_m_a2), 0.0)
        u = root * (gate_i * xs)
        for g in range(SUBLANES):
            a_ref[j, pl.ds(g, seg, stride=SUBLANES), :] = a[g * seg:(g + 1) * seg]
            u_ref[j, pl.ds(g, seg, stride=SUBLANES), :] = u[g * seg:(g + 1) * seg]

    def body(t, carry):
        h, prod = carry
        r0 = pl.multiple_of(t * SUBLANES, SUBLANES)
        a = a_ref[:, pl.ds(r0, SUBLANES), :]
        h = a * h + u_ref[:, pl.ds(r0, SUBLANES), :]
        prod = a * prod
        u_ref[:, pl.ds(r0, SUBLANES), :] = h
        a_ref[:, pl.ds(r0, SUBLANES), :] = prod
        return h, prod

    shape = (n_slabs, SUBLANES, LANES)
    h_end, p_end = lax.fori_loop(0, seg, body, (jnp.zeros(shape, F32), jnp.ones(shape, F32)))

    cur = hprev_ref[...]
    h_in = []
    for g in range(SUBLANES):
        h_in.append(cur)
        cur = h_end[:, g:g + 1] + p_end[:, g:g + 1] * cur
    hprev_ref[...] = cur

    for j in range(n_slabs):
        sl = slice(j * LANES, (j + 1) * LANES)
        for g in range(SUBLANES):
            rows = slice(g * seg, (g + 1) * seg)
            h = (u_ref[j, pl.ds(g, seg, stride=SUBLANES), :]
                 + a_ref[j, pl.ds(g, seg, stride=SUBLANES), :] * h_in[g][j])
            x = gr_ref[0, rows, sl].astype(F32)
            tanh = jnp.tanh(x * (GELU_C0 + GELU_C1 * (x * x)))
            o_ref[0, rows, sl] = ((0.5 * (h * x)) * (1.0 + tanh)).astype(BF16)


def _rglru(p, conv_w, conv_b, w_ri, b_r, b_i, lam):
    bsz, t, _ = p.shape
    width = conv_w.shape[1]
    tt = min(ROW_TILE, t)
    n_slabs = width // LANES
    return pl.pallas_call(
        _rglru_kernel,
        grid=(bsz, t // tt),
        in_specs=[pl.BlockSpec((1, tt, width), lambda b, i: (b, i, 0)),
                  pl.BlockSpec((1, tt, width), lambda b, i: (b, i, 1)),
                  _resident(conv_w.shape), _resident((1, width)), _resident(w_ri.shape),
                  _resident((1, width)), _resident((1, width)), _resident((1, width))],
        out_specs=pl.BlockSpec((1, tt, width), lambda b, i: (b, i, 0)),
        out_shape=jax.ShapeDtypeStruct((bsz, t, width), BF16),
        scratch_shapes=[pltpu.VMEM((SUBLANES, width), F32), pltpu.VMEM((n_slabs, 1, LANES), F32),
                        pltpu.VMEM((n_slabs, tt, LANES), F32), pltpu.VMEM((n_slabs, tt, LANES), F32)],
        compiler_params=_params("arbitrary", "arbitrary"),
        name="rglru",
    )(p, p, conv_w, conv_b, w_ri, b_r, b_i, lam)


def _compress_kernel(r_ref, pe_ref, w1_ref, w2_ref, o_ref, *, n_blk):
    tm, half = r_ref.shape
    n_in_group = lax.broadcasted_iota(jnp.int32, (tm, 1), 0) & (n_blk - 1)
    r = r_ref[...].astype(F32)
    pe = pe_ref[0]
    top = jnp.dot((r + pe[0:1]).astype(BF16), w1_ref[0, 0:half], preferred_element_type=F32)
    bot = jnp.dot((r + pe[1:2]).astype(BF16), w1_ref[0, half:], preferred_element_type=F32)
    nxt = pltpu.roll(bot, tm - 1, axis=0)
    hid = top + jnp.where(n_in_group == n_blk - 1, 0.0, nxt)
    o_ref[...] = jnp.dot(jax.nn.gelu(hid).astype(BF16), w2_ref[0], preferred_element_type=F32).astype(BF16)


def _compress(rows_kv, pe, w1, w2, n_blk):
    rows, half = rows_kv.shape
    tm = NSA_KV_HEADS * n_blk
    kv = lambda i: (i % 2, 0, 0)
    return pl.pallas_call(
        functools.partial(_compress_kernel, n_blk=n_blk),
        grid=(rows // tm,),
        in_specs=[pl.BlockSpec((tm, half), lambda i: (i, 0)),
                  pl.BlockSpec((1,) + pe.shape[1:], kv), pl.BlockSpec((1,) + w1.shape[1:], kv),
                  pl.BlockSpec((1,) + w2.shape[1:], kv)],
        out_specs=pl.BlockSpec((tm, HEAD_DIM), lambda i: (i, 0)),
        out_shape=jax.ShapeDtypeStruct((rows, HEAD_DIM), BF16),
        compiler_params=_params("arbitrary"),
        name="nsa_compress",
    )(rows_kv, pe, w1, w2)


LOG2E = 1.4426950408889634
V_ROWS = HEAD_DIM + 2 * SUBLANES


def _rope(x, cos, sin_lo, sin_hi, axis):
    half = ROPE_DIM // 2
    return (x * cos + pltpu.roll(x, half, axis=axis) * sin_hi
            + pltpu.roll(x, x.shape[axis] - half, axis=axis) * sin_lo)


def _online_softmax(s, vt_tile, m_old, acc):
    m_new = jnp.maximum(m_old, jnp.max(s, axis=0, keepdims=True))
    p = jnp.exp2(s - m_new).astype(BF16)
    return m_new, jnp.exp2(m_old - m_new) * acc + jnp.dot(vt_tile, p, preferred_element_type=F32)


def _nsa_kernel(q_ref, kv_ref, kc_ref, vct_ref, cosq_ref, sloq_ref, shiq_ref, cosk_ref, slok_ref, shik_ref, o_ref,
                ks_s, vst_s, kw_s, vwt_s, s_buf, p_buf, m_buf, a_buf, acc_buf):
    tq = q_ref.shape[1]
    hd, rep = HEAD_DIM, GQA_REP
    n_cmp = kc_ref.shape[2]
    n_slc = ks_s.shape[0] // SLC_BLOCK
    rows = rep * tq
    qi = pl.program_id(2)
    t0 = pl.multiple_of(qi * tq, tq)
    lane = lax.broadcasted_iota(jnp.int32, (1, LANES), 1)
    first = lane < hd

    kv = kv_ref[0]
    cos_k, slo_k, shi_k = cosk_ref[...], slok_ref[...], shik_ref[...]
    key_blk = (t0 + lax.broadcasted_iota(jnp.int32, (tq, LANES), 0)) >> SLC_SHIFT
    blk_onehot = jnp.where(key_blk == lane - hd, 1.0, 0.0)

    def park(pair, extra, k_s, vt_s):
        pair = _rope(pair.astype(F32), cos_k, slo_k, shi_k, 1)
        k_s[pl.ds(t0, tq), :] = jnp.where(first, pair, extra).astype(BF16)
        vt_s[qi] = jnp.where(first, pltpu.roll(pair, hd, axis=1), 1.0).T[0:V_ROWS].astype(BF16)

    park(kv[:, 0:LANES], blk_onehot, ks_s, vst_s)
    park(kv[:, LANES:2 * LANES], 0.0, kw_s, vwt_s)

    q_t = q_ref[0].astype(F32).T * (hd ** -0.5 * LOG2E)
    q_rot_t = _rope(q_t, cosq_ref[...], sloq_ref[...], shiq_ref[...], 0)
    heads_on_lanes = lambda x: jnp.concatenate([x[r * hd:(r + 1) * hd] for r in range(rep)], axis=1)
    t_q = t0 + lax.broadcasted_iota(jnp.int32, (1, tq), 1)
    t_q4 = jnp.concatenate([t_q] * rep, axis=1)
    gate = _sigmoid(kv[:, 2 * LANES:3 * LANES].astype(F32).T[0:2 * SUBLANES])

    w_win = jnp.concatenate([heads_on_lanes(q_rot_t), jnp.zeros((LANES - hd, rows), F32)], axis=0).astype(BF16)
    key_i = lax.broadcasted_iota(jnp.int32, (tq, tq), 0)
    qry_i = lax.broadcasted_iota(jnp.int32, (tq, tq), 1)
    causal = jnp.concatenate([jnp.where(key_i <= qry_i, 0.0, NEG)] * rep, axis=1)
    older = jnp.concatenate([jnp.where(key_i > qry_i, 0.0, NEG)] * rep, axis=1)
    scores = lambda k_s, k0, wq: jnp.dot(k_s[pl.ds(pl.multiple_of(k0, tq), tq), :], wq, preferred_element_type=F32)
    m0 = jnp.full((1, rows), NEG, F32)
    acc0 = jnp.zeros((V_ROWS, rows), F32)

    prev = jnp.maximum(qi - 1, 0)
    s_old = scores(kw_s, prev * tq, w_win)
    s_new = scores(kw_s, t0, w_win)
    s = jnp.dot(kc_ref[0, 0], heads_on_lanes(q_t).astype(BF16), preferred_element_type=F32)

    cmp_end = lax.broadcasted_iota(jnp.int32, (n_cmp, rows), 0) * CMP_STRIDE + (CMP_BLOCK - 1)
    mask_c = cmp_end <= t_q4
    s = jnp.where(mask_c, s, NEG)
    e = jnp.where(mask_c, jnp.exp2(s - jnp.max(s, axis=0, keepdims=True)), 0.0)
    den = jnp.sum(e, axis=0, keepdims=True)
    p = e * jnp.where(den > 0.0, 1.0 / den, 0.0)
    o_cmp = jnp.dot(vct_ref[0, 0], p.astype(BF16), preferred_element_type=F32)
    p_sum = p[:, 0:tq]
    for r in range(1, rep):
        p_sum = p_sum + p[:, r * tq:(r + 1) * tq]

    n_pos = lax.broadcasted_iota(jnp.int32, (n_slc, n_cmp), 1) * CMP_STRIDE
    j_pos = lax.broadcasted_iota(jnp.int32, (n_slc, n_cmp), 0) * SLC_BLOCK
    overlap = jnp.where((n_pos < j_pos + SLC_BLOCK) & (n_pos + CMP_BLOCK > j_pos), 1.0, 0.0)
    imp = jnp.dot(overlap, p_sum, preferred_element_type=F32, precision=lax.Precision.HIGHEST)
    blk = lax.broadcasted_iota(jnp.int32, (n_slc, tq), 0)
    imp = jnp.where((blk == 0) | (blk == (t_q >> SLC_SHIFT)), imp + FORCE_BONUS, imp)
    imp = jnp.where(blk * SLC_BLOCK <= t_q, imp, NEG)
    blk_f = blk.astype(F32)
    drop = jnp.full((n_slc, tq), NEG, F32)
    for _ in range(SLC_TOPK):
        mx = jnp.max(imp, axis=0, keepdims=True)
        lowest = jnp.min(jnp.where(imp == mx, blk_f, float(n_slc)), axis=0, keepdims=True)
        pick = blk_f == lowest
        drop = jnp.where(pick, 0.0, drop)
        imp = jnp.where(pick, -jnp.inf, imp)

    w = jnp.concatenate([heads_on_lanes(q_rot_t), jnp.concatenate([drop] * rep, axis=1),
                         jnp.zeros((LANES - hd - n_slc, rows), F32)], axis=0).astype(BF16)

    m_win, acc_win = _online_softmax(s_old + jnp.where(qi > 0, older, NEG), vwt_s[prev], m0, acc0)
    _, acc_win = _online_softmax(s_new + causal, vwt_s[qi], m_win, acc_win)

    def softmax_stage(s, m_old):
        m_new = jnp.maximum(m_old, jnp.max(s, axis=0, keepdims=True))
        return jnp.exp2(s - m_new).astype(BF16), jnp.exp2(m_old - m_new), m_new

    def value_stage(kt):
        acc_buf[...] = a_buf[...] * acc_buf[...] + jnp.dot(vst_s[kt], p_buf[...], preferred_element_type=F32)

    def sel_step(kt, carry):
        s_next = scores(ks_s, (kt + 1) * tq, w)
        value_stage(jnp.maximum(kt - 1, 0))
        p_buf[...], a_buf[...], m_buf[...] = softmax_stage(s_buf[...], m_buf[...])
        s_buf[...] = s_next
        return carry

    s_buf[...] = scores(ks_s, 0, w)
    m_buf[...] = m0
    acc_buf[...] = acc0
    a_buf[...] = jnp.ones(a_buf.shape, F32)
    p_buf[...] = jnp.zeros(p_buf.shape, BF16)
    lax.fori_loop(0, qi, sel_step, 0)
    value_stage(jnp.maximum(qi - 1, 0))
    p_buf[...], a_buf[...], _ = softmax_stage(s_buf[...] + causal, m_buf[...])
    value_stage(qi)
    acc_sel = acc_buf[...]

    outs = []
    for r in range(rep):
        ln = slice(r * tq, (r + 1) * tq)
        g = [gate[3 * r + c:3 * r + c + 1] for c in range(3)]
        outs.append(g[0] * o_cmp[:, ln] + (g[1] / acc_sel[hd:hd + 1, ln]) * acc_sel[0:hd, ln]
                    + (g[2] / acc_win[hd:hd + 1, ln]) * acc_win[0:hd, ln])
    for h in range(rep // 2):
        pair = jnp.concatenate([outs[2 * h], outs[2 * h + 1]], axis=0)
        o_ref[0, :, h * LANES:(h + 1) * LANES] = pair.T.astype(BF16)


def _nsa(p, k_cmp, v_cmp_t, tables_q, tables_k):
    bsz, t, _ = p.shape
    g = NSA_KV_HEADS
    tq = min(Q_TILE, t // 2)
    qw = GQA_REP * HEAD_DIM
    assert WINDOW == tq and t % tq == 0 and SLC_TOPK <= t // SLC_BLOCK <= LANES - HEAD_DIM and 2 * HEAD_DIM == LANES
    tab_q = pl.BlockSpec((qw, tq), lambda b, h, i: (0, i))
    tab_k = pl.BlockSpec((tq, LANES), lambda b, h, i: (i, 0))
    per_group = lambda a: pl.BlockSpec((1, 1) + a.shape[2:], lambda b, h, i: (b, h, 0, 0))
    keys = pltpu.VMEM((t, LANES), BF16)
    vals = pltpu.VMEM((t // tq, V_ROWS, tq), BF16)
    return pl.pallas_call(
        _nsa_kernel,
        grid=(bsz, g, t // tq),
        in_specs=[pl.BlockSpec((1, tq, qw), lambda b, h, i: (b, i, Q_COL0 // qw + h)),
                  pl.BlockSpec((1, tq, GROUP_COLS), lambda b, h, i: (b, i, GROUP_COL0 // GROUP_COLS + h)),
                  per_group(k_cmp), per_group(v_cmp_t), tab_q, tab_q, tab_q, tab_k, tab_k, tab_k],
        out_specs=pl.BlockSpec((1, tq, qw), lambda b, h, i: (b, i, h)),
        out_shape=jax.ShapeDtypeStruct((bsz, t, NSA_HEADS * HEAD_DIM), BF16),
        scratch_shapes=[keys, vals, keys, vals,
                        pltpu.VMEM((tq, GQA_REP * tq), F32), pltpu.VMEM((tq, GQA_REP * tq), BF16),
                        pltpu.VMEM((1, GQA_REP * tq), F32), pltpu.VMEM((1, GQA_REP * tq), F32),
                        pltpu.VMEM((V_ROWS, GQA_REP * tq), F32)],
        compiler_params=_params("arbitrary", "arbitrary", "arbitrary"),
        name="nsa",
    )(p, p, k_cmp, v_cmp_t, *tables_q, *tables_k)


def _out_kernel(x_ref, ya_ref, yb_ref, mod_ref, gain_ref, wa_ref, wb_ref, o_ref):
    y = (jnp.dot(ya_ref[0], wa_ref[...], preferred_element_type=F32)
         + jnp.dot(yb_ref[0], wb_ref[...], preferred_element_type=F32))
    m = mod_ref[0]
    o_ref[0] = x_ref[0] + (1.0 + m[2:3]) * _rms(y, gain_ref[...])


def _out_proj(x, ya, yb, mod, gain, wa, wb):
    bsz, t, d = x.shape
    tm = min(ROW_TILE, t)
    row = lambda w: pl.BlockSpec((1, tm, w), lambda b, i: (b, i, 0))
    return pl.pallas_call(
        _out_kernel,
        grid=(bsz, t // tm),
        in_specs=[row(d), row(ya.shape[2]), row(yb.shape[2]), pl.BlockSpec((1, 6, d), lambda b, i: (b, 0, 0)),
                  _resident((1, d)), _resident(wa.shape), _resident(wb.shape)],
        out_specs=row(d),
        out_shape=jax.ShapeDtypeStruct((bsz, t, d), F32),
        compiler_params=_params("arbitrary", "arbitrary"),
        name="mixer_out",
    )(x, ya, yb, mod, gain, wa, wb)


def _ab_column_order():
    hd, g = HEAD_DIM, NSA_KV_HEADS
    kv0 = GROUP_COL0
    src = {name: kv0 + n * g * hd for n, name in enumerate(("kc", "vc", "ks", "vs", "kw", "vw"))}
    gates0 = kv0 + 6 * g * hd
    n_gate = 3 * GQA_REP
    cols = list(range(GROUP_COL0))
    for grp in range(g):
        for name in ("ks", "vs", "kw", "vw"):
            cols += list(range(src[name] + grp * hd, src[name] + (grp + 1) * hd))
        cols += list(range(gates0 + grp * n_gate, gates0 + (grp + 1) * n_gate)) + [-1] * (LANES - n_gate)
    cols += list(range(src["kc"], src["kc"] + g * hd)) + list(range(src["vc"], src["vc"] + g * hd))
    return np.asarray(cols, np.int32)


def _rope_tables(t):
    half = ROPE_DIM // 2
    inv = jnp.power(jnp.float32(ROPE_THETA), -jnp.arange(half, dtype=F32) / half)
    ang = jnp.arange(t, dtype=F32)[:, None] * inv[None, :]
    cos, sin = jnp.cos(ang), jnp.sin(ang)
    rest = HEAD_DIM - ROPE_DIM
    one = jnp.ones((t, rest), F32)
    zero = jnp.zeros((t, rest), F32)
    zh = jnp.zeros((t, half), F32)
    head = (jnp.concatenate([cos, cos, one], axis=1),
            jnp.concatenate([-sin, zh, zero], axis=1),
            jnp.concatenate([zh, sin, zero], axis=1))
    ident = (jnp.ones((t, HEAD_DIM), F32), jnp.zeros((t, HEAD_DIM), F32), jnp.zeros((t, HEAD_DIM), F32))
    tables_q = tuple(jnp.tile(a, (1, GQA_REP)).T for a in head)
    tables_k = tuple(jnp.concatenate([a, i], axis=1) for a, i in zip(head, ident))
    return tables_q, tables_k


def _gate_weights(w_r, w_i):
    nb, bw, _ = w_r.shape
    per = LANES // bw
    def slabs(w):
        w = w.reshape(nb // per, per, bw, bw)
        eye = jnp.eye(per, dtype=w.dtype)
        return jnp.einsum("spcd,pq->spcqd", w, eye).reshape(nb // per, LANES, LANES)
    return jnp.concatenate([slabs(w_r), slabs(w_i)], axis=2).astype(BF16)


def kernel(x, c, mod_w, mod_b, norm_mix_pre, norm_mix_post, norm_ffn_pre, norm_ffn_post, ab_w_in, ab_conv_w,
           ab_conv_b, lru_w_r, lru_b_r, lru_w_i, lru_b_i, lru_lam, cmp_pe_k, cmp_pe_v, cmp_wk1, cmp_wk2, cmp_wv1,
           cmp_wv2, ab_w_out, sc_w_in, sc_conv_w, sc_conv_b, sc_w_out, ffn_w_gate, ffn_w_up, ffn_conv_w,
           ffn_conv_b, ffn_w_down):
    bsz, t, d = x.shape
    depth = mod_w.shape[0]
    g, hd = NSA_KV_HEADS, HEAD_DIM
    n_blk = t // CMP_STRIDE
    row2 = lambda a: a.reshape(1, -1)

    mod_all = _modulation(c, mod_w, mod_b).reshape(depth, bsz, 6, d)
    order = _ab_column_order()
    tables_q, tables_k = _rope_tables(t)
    half_blk = CMP_STRIDE * hd

    for i in range(depth):
        mod = mod_all[i]
        if i % 2 == 0:
            e = i // 2
            w_in = jnp.concatenate([ab_w_in[e], jnp.zeros((d, 1), F32)], axis=1)[:, order].astype(BF16)
            p, cmp_in = _proj(x, mod, row2(norm_mix_pre[i]), w_in)
            y_rnn = _rglru(p, ab_conv_w[e], row2(ab_conv_b[e]), _gate_weights(lru_w_r[e], lru_w_i[e]),
                           row2(lru_b_r[e]), row2(lru_b_i[e]), row2(lru_lam[e]))
            stack2 = lambda a, b: jnp.stack([a, b])
            kv_cmp = _compress(cmp_in.reshape(bsz * 2 * g * n_blk, half_blk),
                               stack2(cmp_pe_k[e], cmp_pe_v[e]).reshape(2, 2, half_blk),
                               stack2(cmp_wk1[e], cmp_wv1[e]).astype(BF16),
                               stack2(cmp_wk2[e], cmp_wv2[e]).astype(BF16), n_blk)
            kv_cmp = kv_cmp.reshape(bsz, 2, g, n_blk, hd)
            y_att = _nsa(p, kv_cmp[:, 0], kv_cmp[:, 1].transpose(0, 1, 3, 2), tables_q, tables_k)
            w_out = ab_w_out[e].astype(BF16)
            x = _out_proj(x, y_rnn, y_att, mod, row2(norm_mix_post[i]), w_out[:y_rnn.shape[2]], w_out[y_rnn.shape[2]:])
        else:
            o = i // 2
            w_in = sc_w_in[o].astype(BF16)
            f = sc_w_out.shape[1]
            x = _gated_conv_block(x, mod, row2(norm_mix_pre[i]), row2(norm_mix_post[i]),
                                  [w_in[:, :f], w_in[:, f:2 * f], w_in[:, 2 * f:]],
                                  sc_conv_w[o], row2(sc_conv_b[o]), sc_w_out[o].astype(BF16), ffn=False)
        x = _gated_conv_block(x, mod, row2(norm_ffn_pre[i]), row2(norm_ffn_post[i]),
                              [ffn_w_gate[i].astype(BF16), ffn_w_up[i].astype(BF16)],
                              ffn_conv_w[i], row2(ffn_conv_b[i]), ffn_w_down[i].astype(BF16), ffn=True)
    return x
```
